```python
import jax
import jax.numpy as jnp
from jax import lax
import numpy as np

D_MODEL = 4096
BATCH = 4
SEQ = 2048
DEPTH = 2
DEC_BATCH = 8
DEC_SEQ = 8
PAST_LEN = 16384
PAGE_SIZE = 128

N_GROUPS = 4
GROUP_W = D_MODEL // N_GROUPS
HEAD_DIM = 128
GLA_HEADS = GROUP_W // HEAD_DIM
GLA_DK = HEAD_DIM
GLA_DV = HEAD_DIM
GLA_GATE_RANK = 16
GLA_TAU = 16.0
GLA_CHUNK = 64
FOX_HEADS = GROUP_W // HEAD_DIM
Q_BLOCK = 128
MOBA_HEADS = GROUP_W // HEAD_DIM
MOBA_BLOCK = 256
MOBA_TOPK = 3
MOBA_Q_CHUNK = 16
ROPE_THETA = 10000.0
RWKV_HEAD = 64
RWKV_HEADS = GROUP_W // RWKV_HEAD
RWKV_W_RANK = 64
RWKV_A_RANK = 64
RWKV_G_RANK = 160
RWKV_LN_EPS = 64e-5
PEER_HEADS = 8
PEER_NKEYS = 128
PEER_EXPERTS = PEER_NKEYS * PEER_NKEYS
PEER_KEY_DIM = 256
PEER_HALF = PEER_KEY_DIM // 2
PEER_TOPK = 16
PEER_TOKEN_BLOCK = 64
GLA_COLS = 4 * GROUP_W + GLA_GATE_RANK
FOX_COLS = 3 * GROUP_W + FOX_HEADS
MOBA_COLS = 3 * GROUP_W
RWKV_COLS = 3 * GROUP_W + RWKV_W_RANK + RWKV_A_RANK + RWKV_G_RANK
GLA_OFF = 0
FOX_OFF = GLA_OFF + GLA_COLS
MOBA_OFF = FOX_OFF + FOX_COLS
RWKV_OFF = MOBA_OFF + MOBA_COLS
N_IN = RWKV_OFF + RWKV_COLS
N_MOD = 6
RMS_EPS = 1e-6
F32 = jnp.float32

kernel_name = 'hymba_style_gla_fox_moba_rwkv7_peer_step'


def _rmsnorm(x, gain):
    xf = x.astype(F32)
    y = xf * lax.rsqrt(jnp.mean(xf * xf, axis=-1, keepdims=True) + RMS_EPS)
    return (y * gain.astype(F32)).astype(x.dtype)


def _heads(t, n):
    return t.reshape(t.shape[:-1] + (n, -1))


def _head_rmsnorm(o, gain):
    on = o * lax.rsqrt(jnp.mean(o * o, axis=-1, keepdims=True) + RMS_EPS)
    return on.reshape(o.shape[0], o.shape[1], -1) * gain.astype(F32)


def _rope(x, pos):
    hd = x.shape[-1]
    inv = 1.0 / (ROPE_THETA ** (jnp.arange(0, hd, 2, dtype=F32) / hd))
    ang = pos.astype(F32)[:, None] * inv[None, :]
    cos = jnp.cos(ang)[None, :, None, :]
    sin = jnp.sin(ang)[None, :, None, :]
    xf = x.astype(F32)
    x1, x2 = xf[..., :hd // 2], xf[..., hd // 2:]
    return jnp.concatenate([x1 * cos - x2 * sin, x2 * cos + x1 * sin], axis=-1).astype(x.dtype)


def _gla(q, k, v, log_a, s0):
    B, T, H, _ = q.shape
    pad = (-T) % GLA_CHUNK
    nc = (T + pad) // GLA_CHUNK

    def prep(t):
        t = jnp.pad(t.astype(F32), ((0, 0), (0, pad), (0, 0), (0, 0)))
        return t.reshape(B, nc, GLA_CHUNK, H, t.shape[-1]).transpose(1, 0, 3, 2, 4)

    causal = jnp.tril(jnp.ones((GLA_CHUNK, GLA_CHUNK), dtype=bool))

    def step(s, inp):
        qc, kc, vc, ac = inp
        b = jnp.cumsum(ac, axis=2)
        diff = jnp.where(causal[:, :, None], b[:, :, :, None, :] - b[:, :, None, :, :], -jnp.inf)
        attn = jnp.einsum('bhtsd,bhsd->bhts', jnp.exp(diff) * qc[:, :, :, None, :], kc)
        o = attn @ vc + jnp.einsum('bhtd,bhde->bhte', qc * jnp.exp(b), s)
        b_last = b[:, :, -1:, :]
        s = jnp.exp(b_last[:, :, 0, :])[..., None] * s + jnp.einsum('bhsd,bhse->bhde', kc * jnp.exp(b_last - b), vc)
        return s, o

    s, o = lax.scan(step, s0.astype(F32), (prep(q), prep(k), prep(v), prep(log_a)))
    o = o.transpose(1, 0, 3, 2, 4).reshape(B, nc * GLA_CHUNK, H, -1)[:, :T]
    return o, s


def _fox_attention(q, k, v, q_pos, q_cum, k_cum):
    B, Tq, H, hd = q.shape
    Tk = k.shape[1]
    qb = min(Q_BLOCK, Tq)
    pad = (-Tq) % qb
    nb = (Tq + pad) // qb
    qp = jnp.pad(q.astype(F32), ((0, 0), (0, pad), (0, 0), (0, 0))).reshape(B, nb, qb, H, hd).transpose(1, 0, 2, 3, 4)
    pp = jnp.pad(q_pos, (0, pad), mode='edge').reshape(nb, qb)
    cp = jnp.pad(q_cum, ((0, 0), (0, pad), (0, 0)), mode='edge').reshape(B, nb, qb, H).transpose(1, 0, 3, 2)
    kf = k.astype(F32)
    vf = v.astype(F32)
    kc = k_cum.transpose(0, 2, 1)
    k_pos = jnp.arange(Tk, dtype=jnp.int32)
    scale = hd ** -0.5

    def block(args):
        qc, pc, cc = args
        s = jnp.einsum('bqhd,bkhd->bhqk', qc, kf) * scale + cc[..., None] - kc[:, :, None, :]
        s = jnp.where(k_pos[None, None, None, :] <= pc[None, None, :, None], s, -jnp.inf)
        return jnp.einsum('bhqk,bkhd->bqhd', jax.nn.softmax(s, axis=-1), vf)

    o = lax.map(block, (qp, pp, cp))
    return o.transpose(1, 0, 2, 3, 4).reshape(B, nb * qb, H, hd)[:, :Tq]


def _moba_attention(q, k, v, q_pos):
    B, Tq, H, hd = q.shape
    Tk = k.shape[1]
    nblk = max(-(-Tk // MOBA_BLOCK), MOBA_TOPK)
    kpad = nblk * MOBA_BLOCK - Tk
    kb = jnp.pad(k.astype(F32), ((0, 0), (0, kpad), (0, 0), (0, 0))).reshape(B, nblk, MOBA_BLOCK, H, hd)
    vb = jnp.pad(v.astype(F32), ((0, 0), (0, kpad), (0, 0), (0, 0))).reshape(B, nblk, MOBA_BLOCK, H, hd)
    k_mean = jnp.mean(kb, axis=2)
    kb = kb.transpose(0, 3, 1, 2, 4)
    vb = vb.transpose(0, 3, 1, 2, 4)
    qc_n = min(MOBA_Q_CHUNK, Tq)
    pad = (-Tq) % qc_n
    nq = (Tq + pad) // qc_n
    qp = jnp.pad(q.astype(F32), ((0, 0), (0, pad), (0, 0), (0, 0))).reshape(B, nq, qc_n, H, hd).transpose(1, 0, 2, 3, 4)
    pp = jnp.pad(q_pos, (0, pad), mode='edge').reshape(nq, qc_n)
    blk_ids = jnp.arange(nblk, dtype=jnp.int32)
    slot_ids = jnp.arange(MOBA_TOPK + 1, dtype=jnp.int32)
    in_blk = jnp.arange(MOBA_BLOCK, dtype=jnp.int32)
    gather = jax.vmap(jax.vmap(lambda t, i: t[i]))
    scale = hd ** -0.5

    def chunk(args):
        qc, pc = args
        own = pc // MOBA_BLOCK
        g = jnp.einsum('bqhd,bnhd->bhqn', qc, k_mean)
        g = jnp.where(blk_ids[None, None, None, :] < own[None, None, :, None], g, -jnp.inf)
        _, sel = lax.top_k(g, MOBA_TOPK)
        own_b = jnp.broadcast_to(own[None, None, :, None], sel.shape[:3] + (1,)).astype(sel.dtype)
        blocks = jnp.concatenate([sel, own_b], axis=-1)
        kg = gather(kb, blocks)
        vg = gather(vb, blocks)
        key_pos = blocks[..., None] * MOBA_BLOCK + in_blk
        slot_ok = (slot_ids[None, :] < own[:, None]) | (slot_ids[None, :] == MOBA_TOPK)
        ok = slot_ok[None, None, :, :, None] & (key_pos <= pc[None, None, :, None, None])
        s = jnp.einsum('bqhd,bhqjkd->bhqjk', qc, kg) * scale
        s = jnp.where(ok, s, -jnp.inf)
        shp = s.shape
        p = jax.nn.softmax(s.reshape(shp[0], shp[1], shp[2], -1), axis=-1).reshape(shp)
        return jnp.einsum('bhqjk,bhqjkd->bqhd', p, vg)

    o = lax.map(chunk, (qp, pp))
    return o.transpose(1, 0, 2, 3, 4).reshape(B, nq * qc_n, H, hd)[:, :Tq]


def _rwkv_scan(r, logw, k, v, kk, a, s0):
    def step(s, inp):
        r_t, w_t, k_t, v_t, kk_t, a_t = inp
        sa = jnp.einsum('bhvk,bhk->bhv', s, -kk_t)
        s = (s * jnp.exp(w_t)[:, :, None, :] + sa[..., None] * (kk_t * a_t)[:, :, None, :]
             + v_t[..., None] * k_t[:, :, None, :])
        return s, jnp.einsum('bhvk,bhk->bhv', s, r_t)

    xs = tuple(t.astype(F32).transpose(1, 0, 2, 3) for t in (r, logw, k, v, kk, a))
    s, y = lax.scan(step, s0.astype(F32), xs)
    return y.transpose(1, 0, 2, 3), s


def _peer(h, wq, keys, u, v):
    B, T, D = h.shape
    n = B * T
    tb = min(PEER_TOKEN_BLOCK, n)
    pad = (-n) % tb
    hf = jnp.pad(h.reshape(n, D), ((0, pad), (0, 0))).reshape(-1, tb, D)
    keys_f = keys.astype(F32)

    def block(hb):
        q = (hb @ wq).astype(F32).reshape(tb, PEER_HEADS, 2, PEER_HALF)
        s = jnp.einsum('nhpd,hpkd->nhpk', q, keys_f)
        s1, i1 = lax.top_k(s[:, :, 0], PEER_TOPK)
        s2, i2 = lax.top_k(s[:, :, 1], PEER_TOPK)
        cand = (s1[..., :, None] + s2[..., None, :]).reshape(tb, PEER_HEADS, -1)
        cidx = (i1[..., :, None] * PEER_NKEYS + i2[..., None, :]).reshape(tb, PEER_HEADS, -1)
        top, pos = lax.top_k(cand, PEER_TOPK)
        eidx = jnp.take_along_axis(cidx, pos, axis=-1)
        gate = jax.nn.softmax(top, axis=-1)
        act = jax.nn.gelu(jnp.einsum('nd,nhkd->nhk', hb.astype(F32), u[eidx].astype(F32)), approximate=False)
        return jnp.einsum('nhk,nhkd->nd', gate * act, v[eidx].astype(F32)).astype(h.dtype)

    out = lax.map(block, hf).reshape(-1, D)[:n]
    return out.reshape(B, T, D)


def _layer(x, c_act, l, p, fox_k0, fox_v0, fox_lf0, moba_k0, moba_v0, gla_s0, rwkv_s0, shift0):
    B, T, _ = x.shape
    dt = x.dtype
    P = fox_k0.shape[1]
    pos = P + jnp.arange(T, dtype=jnp.int32)
    mod = (c_act @ p['w_mod'][l] + p['b_mod'][l])[:, None, :]
    sh1, sc1, ga1, sh2, sc2, ga2 = jnp.split(mod, N_MOD, axis=-1)
    h = _rmsnorm(x, p['norm_mix'][l]) * (1 + sc1) + sh1
    proj = h @ p['w_in'][l]

    gp = proj[..., GLA_OFF:GLA_OFF + GLA_COLS]
    gq = _heads(gp[..., :GROUP_W], GLA_HEADS)
    gk = _heads(gp[..., GROUP_W:2 * GROUP_W], GLA_HEADS)
    gv = _heads(gp[..., 2 * GROUP_W:3 * GROUP_W], GLA_HEADS)
    gg = gp[..., 3 * GROUP_W:4 * GROUP_W].astype(F32)
    log_a = jax.nn.log_sigmoid((gp[..., 4 * GROUP_W:] @ p['gla_gate_up'][l] + p['gla_gate_b'][l]).astype(F32)) / GLA_TAU
    o, gla_s = _gla(gq * GLA_DK ** -0.5, gk, gv, _heads(log_a, GLA_HEADS), gla_s0)
    o_gla = _head_rmsnorm(o, p['gla_norm'][l]) * jax.nn.silu(gg)

    fp = proj[..., FOX_OFF:FOX_OFF + FOX_COLS]
    fq = _heads(fp[..., :GROUP_W], FOX_HEADS)
    fk = _heads(fp[..., GROUP_W:2 * GROUP_W], FOX_HEADS)
    fv = _heads(fp[..., 2 * GROUP_W:3 * GROUP_W], FOX_HEADS)
    logf = jax.nn.log_sigmoid(fp[..., 3 * GROUP_W:].astype(F32) + p['fox_forget_b'][l].astype(F32))
    fk_all = jnp.concatenate([fox_k0.astype(dt), fk], axis=1)
    fv_all = jnp.concatenate([fox_v0.astype(dt), fv], axis=1)
    cum = jnp.cumsum(jnp.concatenate([fox_lf0.astype(F32), logf], axis=1), axis=1)
    o_fox = _fox_attention(fq, fk_all, fv_all, pos, cum[:, P:], cum).reshape(B, T, GROUP_W)

    mp = proj[..., MOBA_OFF:MOBA_OFF + MOBA_COLS]
    mq = _rope(_heads(mp[..., :GROUP_W], MOBA_HEADS), pos)
    mk = _rope(_heads(mp[..., GROUP_W:2 * GROUP_W], MOBA_HEADS), pos)
    mv = _heads(mp[..., 2 * GROUP_W:], MOBA_HEADS)
    mk_all = jnp.concatenate([moba_k0.astype(dt), mk], axis=1)
    mv_all = jnp.concatenate([moba_v0.astype(dt), mv], axis=1)
    o_moba = _moba_attention(mq, mk_all, mv_all, pos).reshape(B, T, GROUP_W)

    rp = proj[..., RWKV_OFF:RWKV_OFF + RWKV_COLS]
    prev = jnp.concatenate([shift0[:, None, :].astype(dt), rp[:, :-1]], axis=1)
    xm = (rp + (prev - rp) * p['rwkv_mu'][l]).astype(F32)
    o0 = 3 * GROUP_W
    r = xm[..., :GROUP_W]
    kr = xm[..., GROUP_W:2 * GROUP_W]
    vr = xm[..., 2 * GROUP_W:o0]
    dw = xm[..., o0:o0 + RWKV_W_RANK]
    da = xm[..., o0 + RWKV_W_RANK:o0 + RWKV_W_RANK + RWKV_A_RANK]
    dg = xm[..., o0 + RWKV_W_RANK + RWKV_A_RANK:]
    w_raw = -jax.nn.softplus(-(p['rwkv_w0'][l] + jnp.tanh(dw) @ p['rwkv_w_up'][l])) - 0.5
    logw = -jnp.exp(w_raw.astype(F32))
    a = jax.nn.sigmoid((p['rwkv_a0'][l] + da @ p['rwkv_a_up'][l]).astype(F32))
    g = (jax.nn.sigmoid(dg) @ p['rwkv_g_up'][l]).astype(F32)
    kk = _heads(kr * p['rwkv_k_k'][l], RWKV_HEADS)
    kk = kk / jnp.maximum(jnp.sqrt(jnp.sum(kk * kk, axis=-1, keepdims=True)), 1e-12)
    kr = kr * (1 + (a - 1) * p['rwkv_k_a'][l])
    rh = _heads(r, RWKV_HEADS)
    kh = _heads(kr, RWKV_HEADS).astype(F32)
    vh = _heads(vr, RWKV_HEADS)
    y, rwkv_s = _rwkv_scan(rh, _heads(logw, RWKV_HEADS), kh, vh, kk, _heads(a, RWKV_HEADS), rwkv_s0)
    mu = jnp.mean(y, axis=-1, keepdims=True)
    var = jnp.mean((y - mu) ** 2, axis=-1, keepdims=True)
    yn = ((y - mu) * lax.rsqrt(var + RWKV_LN_EPS)).reshape(B, T, GROUP_W) * p['rwkv_ln_w'][l] + p['rwkv_ln_b'][l]
    bonus = (jnp.sum(rh * kh * p['rwkv_r_k'][l], axis=-1, keepdims=True) * vh).reshape(B, T, GROUP_W)
    o_rwkv = (yn + bonus) * g

    mix = jnp.concatenate([o_gla, o_fox, o_moba, o_rwkv], axis=-1).astype(dt)
    x = x + ga1 * (mix @ p['w_out'][l])
    h2 = _rmsnorm(x, p['norm_ffn'][l]) * (1 + sc2) + sh2
    x = x + ga2 * _peer(h2, p['peer_wq'][l], p['peer_keys'][l], p['peer_u'][l], p['peer_v'][l])
    new = (fk, fv, logf.astype(dt), mk, mv, gla_s.astype(dt), rwkv_s.astype(dt), rp[:, -1])
    return x, new


def _trunk(x, c, p, pasts):
    c_act = jax.nn.silu(c)
    news = []
    for l in range(DEPTH):
        x, new = _layer(x, c_act, l, p, *pasts[l])
        news.append(new)
    y = _rmsnorm(x, p['final_norm'])
    stacked = [jnp.stack([n[i] for n in news]) for i in range(8)]
    return y, stacked


def _gather_pages(pool, page_table):
    g = pool[page_table]
    return g.reshape((g.shape[0], g.shape[1] * g.shape[2]) + g.shape[3:])


def setup_inputs(seed: int = 0) -> dict:
    key = jax.random.key(seed)
    keys = list(jax.random.split(key, 64))

    def nrm(shape, scale):
        return jax.random.normal(keys.pop(), shape, F32) * scale

    n_pages = PAST_LEN // PAGE_SIZE
    n_used = DEC_BATCH * n_pages
    n_pool = n_used + (n_used + 3) // 4
    D = D_MODEL
    x_prompt = nrm((BATCH, SEQ, D), 1.0)
    x_sample = nrm((DEC_BATCH, DEC_SEQ, D), 1.0)
    c_prompt = nrm((BATCH, D), 1.0)
    c_sample = nrm((DEC_BATCH, D), 1.0)
    cache_fox_k = nrm((DEPTH, n_pool, PAGE_SIZE, FOX_HEADS, HEAD_DIM), 1.0)
    cache_fox_v = nrm((DEPTH, n_pool, PAGE_SIZE, FOX_HEADS, HEAD_DIM), 1.0)
    cache_fox_logf = jax.nn.log_sigmoid(nrm((DEPTH, n_pool, PAGE_SIZE, FOX_HEADS), 1.0) + 3.0)
    cache_moba_k = nrm((DEPTH, n_pool, PAGE_SIZE, MOBA_HEADS, HEAD_DIM), 1.0)
    cache_moba_v = nrm((DEPTH, n_pool, PAGE_SIZE, MOBA_HEADS, HEAD_DIM), 1.0)
    state_gla = nrm((DEPTH, DEC_BATCH, GLA_HEADS, GLA_DK, GLA_DV), 0.1)
    state_rwkv = nrm((DEPTH, DEC_BATCH, RWKV_HEADS, RWKV_HEAD, RWKV_HEAD), 0.1)
    state_rwkv_shift = nrm((DEPTH, DEC_BATCH, RWKV_COLS), 1.0)
    page_table = jax.random.permutation(keys.pop(), n_pool)[:n_used].reshape(DEC_BATCH, n_pages).astype(jnp.int32)
    return {
        'x_prompt': x_prompt, 'x_sample': x_sample,
        'cache_fox_k': cache_fox_k, 'cache_fox_v': cache_fox_v, 'cache_fox_logf': cache_fox_logf,
        'cache_moba_k': cache_moba_k, 'cache_moba_v': cache_moba_v,
        'state_gla': state_gla, 'state_rwkv': state_rwkv, 'state_rwkv_shift': state_rwkv_shift,
        'page_table': page_table, 'c_prompt': c_prompt, 'c_sample': c_sample,
        'w_mod': nrm((DEPTH, D, N_MOD * D), 0.5 * D ** -0.5),
        'b_mod': nrm((DEPTH, N_MOD * D), 0.02),
        'norm_mix': 1.0 + nrm((DEPTH, D), 0.01),
        'norm_ffn': 1.0 + nrm((DEPTH, D), 0.01),
        'w_in': nrm((DEPTH, D, N_IN), D ** -0.5),
        'gla_gate_up': nrm((DEPTH, GLA_GATE_RANK, GROUP_W), GLA_GATE_RANK ** -0.5),
        'gla_gate_b': nrm((DEPTH, GROUP_W), 0.1),
        'gla_norm': 1.0 + nrm((DEPTH, GROUP_W), 0.01),
        'fox_forget_b': 2.0 + nrm((DEPTH, FOX_HEADS), 0.1),
        'rwkv_mu': jax.random.uniform(keys.pop(), (DEPTH, RWKV_COLS), F32),
        'rwkv_w0': nrm((DEPTH, GROUP_W), 0.5) - 0.5,
        'rwkv_w_up': nrm((DEPTH, RWKV_W_RANK, GROUP_W), 0.1),
        'rwkv_a0': nrm((DEPTH, GROUP_W), 0.1),
        'rwkv_a_up': nrm((DEPTH, RWKV_A_RANK, GROUP_W), 0.5 * RWKV_A_RANK ** -0.5),
        'rwkv_g_up': nrm((DEPTH, RWKV_G_RANK, GROUP_W), RWKV_G_RANK ** -0.5),
        'rwkv_k_k': 0.85 + nrm((DEPTH, GROUP_W), 0.1),
        'rwkv_k_a': 1.0 + nrm((DEPTH, GROUP_W), 0.1),
        'rwkv_r_k': nrm((DEPTH, RWKV_HEADS, RWKV_HEAD), 0.1),
        'rwkv_ln_w': 1.0 + nrm((DEPTH, GROUP_W), 0.01),
        'rwkv_ln_b': nrm((DEPTH, GROUP_W), 0.01),
        'w_out': nrm((DEPTH, D, D), D ** -0.5),
        'peer_wq': nrm((DEPTH, D, PEER_HEADS * PEER_KEY_DIM), D ** -0.5),
        'peer_keys': nrm((DEPTH, PEER_HEADS, 2, PEER_NKEYS, PEER_HALF), PEER_HALF ** -0.5),
        'peer_u': nrm((DEPTH, PEER_EXPERTS, D), D ** -0.5),
        'peer_v': nrm((DEPTH, PEER_EXPERTS, D), 1.0),
        'final_norm': 1.0 + nrm((D,), 0.01),
    }


def reference(x_prompt, x_sample, cache_fox_k, cache_fox_v, cache_fox_logf, cache_moba_k, cache_moba_v,
              state_gla, state_rwkv, state_rwkv_shift, page_table, c_prompt, c_sample,
              w_mod, b_mod, norm_mix, norm_ffn, w_in, gla_gate_up, gla_gate_b, gla_norm, fox_forget_b,
              rwkv_mu, rwkv_w0, rwkv_w_up, rwkv_a0, rwkv_a_up, rwkv_g_up, rwkv_k_k, rwkv_k_a, rwkv_r_k,
              rwkv_ln_w, rwkv_ln_b, w_out, peer_wq, peer_keys, peer_u, peer_v, final_norm):
    p = dict(w_mod=w_mod, b_mod=b_mod, norm_mix=norm_mix, norm_ffn=norm_ffn, w_in=w_in,
             gla_gate_up=gla_gate_up, gla_gate_b=gla_gate_b, gla_norm=gla_norm, fox_forget_b=fox_forget_b,
             rwkv_mu=rwkv_mu, rwkv_w0=rwkv_w0, rwkv_w_up=rwkv_w_up, rwkv_a0=rwkv_a0, rwkv_a_up=rwkv_a_up,
             rwkv_g_up=rwkv_g_up, rwkv_k_k=rwkv_k_k, rwkv_k_a=rwkv_k_a, rwkv_r_k=rwkv_r_k,
             rwkv_ln_w=rwkv_ln_w, rwkv_ln_b=rwkv_ln_b, w_out=w_out, peer_wq=peer_wq, peer_keys=peer_keys,
             peer_u=peer_u, peer_v=peer_v, final_norm=final_norm)
    dt = x_prompt.dtype
    B = x_prompt.shape[0]
    empty = (jnp.zeros((B, 0, FOX_HEADS, HEAD_DIM), dt), jnp.zeros((B, 0, FOX_HEADS, HEAD_DIM), dt),
             jnp.zeros((B, 0, FOX_HEADS), dt),
             jnp.zeros((B, 0, MOBA_HEADS, HEAD_DIM), dt), jnp.zeros((B, 0, MOBA_HEADS, HEAD_DIM), dt),
             jnp.zeros((B, GLA_HEADS, GLA_DK, GLA_DV), dt),
             jnp.zeros((B, RWKV_HEADS, RWKV_HEAD, RWKV_HEAD), dt),
             jnp.zeros((B, RWKV_COLS), dt))
    prompt_pasts = [empty for _ in range(DEPTH)]
    sample_pasts = [(_gather_pages(cache_fox_k[l], page_table), _gather_pages(cache_fox_v[l], page_table),
                     _gather_pages(cache_fox_logf[l], page_table),
                     _gather_pages(cache_moba_k[l], page_table), _gather_pages(cache_moba_v[l], page_table),
                     state_gla[l], state_rwkv[l], state_rwkv_shift[l]) for l in range(DEPTH)]
    y_prompt, np_ = _trunk(x_prompt, c_prompt, p, prompt_pasts)
    y_sample, ns_ = _trunk(x_sample, c_sample, p, sample_pasts)
    return (y_prompt, y_sample,
            np_[0], np_[1], np_[2], np_[3], np_[4], np_[5], np_[6], np_[7],
            ns_[0], ns_[1], ns_[2], ns_[3], ns_[4], ns_[5], ns_[6], ns_[7])
```

```python
import functools

import jax
import jax.numpy as jnp
from jax import lax
from jax.experimental import pallas as pl
from jax.experimental.pallas import tpu as pltpu

F32 = jnp.float32
BF16 = jnp.bfloat16
HIGHEST = lax.Precision.HIGHEST

LANES = 128
HEAD_DIM = 128
RWKV_HEAD = 64
N_MOD = 6
RMS_EPS = 1e-6
GLA_TAU = 16.0
GLA_CHUNK = 64
GLA_SUB = 16
MOBA_BLOCK = 256
MOBA_TOPK = 3
ROPE_THETA = 10000.0
RWKV_LN_EPS = 64e-5
PEER_HEADS = 8
PEER_TOPK = 16
NEG = -1e30
VMEM_LIMIT = 52 * 1024 * 1024
PAGES_PER_STEP = 8


def _cp(sem, vmem=VMEM_LIMIT):
    return pltpu.CompilerParams(dimension_semantics=sem, vmem_limit_bytes=vmem)


def _pick(n, pref, mult=LANES):
    best = None
    d = mult
    while d <= min(n, pref):
        if n % d == 0:
            best = d
        d += mult
    return best if best is not None else n


def _row_tiles(B, T):
    if T >= 256:
        return 1, _pick(T, 512, 8)
    return B, T


def _log_sigmoid(z):
    return jnp.minimum(z, 0.0) - jnp.log(1.0 + jnp.exp(-jnp.abs(z)))


def _softplus(z):
    return jnp.maximum(z, 0.0) + jnp.log(1.0 + jnp.exp(-jnp.abs(z)))


def _cumsum_rows(x):
    n = x.shape[0]
    row = lax.broadcasted_iota(jnp.int32, x.shape, 0)
    s = 1
    while s < n:
        x = x + jnp.where(row >= s, pltpu.roll(x, s, axis=0), 0.0)
        s *= 2
    return x


def _dot_nt(a, b, **kw):
    return lax.dot_general(a, b, (((1,), (1,)), ((), ())), preferred_element_type=F32, **kw)


def _dot_tn(a, b):
    return lax.dot_general(a, b, (((0,), (0,)), ((), ())), preferred_element_type=F32)


def _mod_kernel(c_ref, w_ref, b_ref, o_ref):
    c = c_ref[...]
    ca = (c * jax.nn.sigmoid(c)).astype(BF16)
    o_ref[...] = jnp.dot(ca, w_ref[...].astype(BF16), preferred_element_type=F32) + b_ref[...]


def _adaln_mod(c_all, w_mod, b_mod):
    L, D, N = w_mod.shape
    R = c_all.shape[0]
    tn = _pick(N, 512)
    return pl.pallas_call(
        _mod_kernel,
        grid=(L, N // tn),
        in_specs=[pl.BlockSpec((R, D), lambda l, n: (0, 0)),
                  pl.BlockSpec((None, D, tn), lambda l, n: (l, 0, n)),
                  pl.BlockSpec((None, 1, tn), lambda l, n: (l, 0, n))],
        out_specs=pl.BlockSpec((None, R, tn), lambda l, n: (l, 0, n)),
        out_shape=jax.ShapeDtypeStruct((L, R, N), F32),
        compiler_params=_cp(("parallel", "parallel")),
        name="adaln_mod",
    )(c_all, w_mod, b_mod.reshape(L, 1, N))


def _nm_kernel(x_ref, sc_ref, sh_ref, gain_ref, w_ref, o_ref, *rest, emit_h):
    if emit_h:
        h_out_ref, h_scr = rest
    else:
        (h_scr,) = rest

    @pl.when(pl.program_id(2) == 0)
    def _():
        x = x_ref[...]
        ms = jnp.mean(x * x, axis=-1, keepdims=True)
        y = x * lax.rsqrt(ms + RMS_EPS) * gain_ref[...]
        h = y * (1.0 + sc_ref[...]) + sh_ref[...]
        hb = h.reshape(h_scr.shape).astype(BF16)
        h_scr[...] = hb
        if emit_h:
            h_out_ref[...] = hb

    o_ref[...] = jnp.dot(h_scr[...], w_ref[...], preferred_element_type=F32)


def _norm_mod_matmul(x3, mod3, sc_idx, sh_idx, gain, w, *, emit_h, name):
    B, T, D = x3.shape
    N = w.shape[1]
    bb, tt = _row_tiles(B, T)
    tn = _pick(N, 512)
    M = B * T
    nt = T // tt
    out_shape = [jax.ShapeDtypeStruct((M, N), F32)]
    out_specs = [pl.BlockSpec((bb * tt, tn), lambda b, t, n: (b * nt + t, n))]
    if emit_h:
        out_shape.append(jax.ShapeDtypeStruct((M, D), BF16))
        out_specs.append(pl.BlockSpec((bb * tt, D), lambda b, t, n: (b * nt + t, 0)))
    res = pl.pallas_call(
        functools.partial(_nm_kernel, emit_h=emit_h),
        grid=(B // bb, nt, N // tn),
        in_specs=[pl.BlockSpec((bb, tt, D), lambda b, t, n: (b, t, 0)),
                  pl.BlockSpec((bb, 1, D), lambda b, t, n: (b, 0, sc_idx)),
                  pl.BlockSpec((bb, 1, D), lambda b, t, n: (b, 0, sh_idx)),
                  pl.BlockSpec((1, D), lambda b, t, n: (0, 0)),
                  pl.BlockSpec((D, tn), lambda b, t, n: (0, n))],
        out_specs=out_specs,
        out_shape=out_shape,
        scratch_shapes=[pltpu.VMEM((bb * tt, D), BF16)],
        compiler_params=_cp(("parallel", "parallel", "arbitrary")),
        name=name,
    )(x3, mod3, mod3, gain.reshape(1, D), w)
    return res if emit_h else res[0]


def _op_kernel(x_ref, m0, m1, m2, m3, w_ref, ga_ref, o_ref):
    G = m0.shape[1]
    acc = jnp.dot(m0[...], w_ref[0:G, :], preferred_element_type=F32)
    acc += jnp.dot(m1[...], w_ref[G:2 * G, :], preferred_element_type=F32)
    acc += jnp.dot(m2[...], w_ref[2 * G:3 * G, :], preferred_element_type=F32)
    acc += jnp.dot(m3[...], w_ref[3 * G:4 * G, :], preferred_element_type=F32)
    o_ref[...] = x_ref[...] + ga_ref[...] * acc.reshape(x_ref.shape)


def _out_proj(x3, mixes, w_out, mod3, ga_idx):
    B, T, D = x3.shape
    G = mixes[0].shape[1]
    bb, tt = _row_tiles(B, T)
    tn = _pick(D, 1024)
    nt = T // tt
    nd = D // tn
    mspec = pl.BlockSpec((bb * tt, G), lambda b, t, n: (b * nt + t, 0))
    return pl.pallas_call(
        _op_kernel,
        grid=(B // bb, nt, nd),
        in_specs=[pl.BlockSpec((bb, tt, tn), lambda b, t, n: (b, t, n)),
                  mspec, mspec, mspec, mspec,
                  pl.BlockSpec((D, tn), lambda b, t, n: (0, n)),
                  pl.BlockSpec((bb, 1, tn), lambda b, t, n: (b, 0, ga_idx * nd + n))],
        out_specs=pl.BlockSpec((bb, tt, tn), lambda b, t, n: (b, t, n)),
        out_shape=jax.ShapeDtypeStruct((B, T, D), F32),
        compiler_params=_cp(("parallel", "parallel", "parallel")),
        name="out_proj",
    )(x3, *mixes, w_out, mod3)


def _fnorm_kernel(x_ref, g_ref, o_ref):
    x = x_ref[...]
    ms = jnp.mean(x * x, axis=-1, keepdims=True)
    o_ref[...] = x * lax.rsqrt(ms + RMS_EPS) * g_ref[...]


def _final_norm(x3, gain):
    B, T, D = x3.shape
    M = B * T
    tm = _pick(M, 512, 8)
    y = pl.pallas_call(
        _fnorm_kernel,
        grid=(M // tm,),
        in_specs=[pl.BlockSpec((tm, D), lambda i: (i, 0)), pl.BlockSpec((1, D), lambda i: (0, 0))],
        out_specs=pl.BlockSpec((tm, D), lambda i: (i, 0)),
        out_shape=jax.ShapeDtypeStruct((M, D), F32),
        compiler_params=_cp(("parallel",)),
        name="final_norm",
    )(x3.reshape(M, D), gain.reshape(1, D))
    return y.reshape(B, T, D)


def _gla_kernel(q_ref, k_ref, v_ref, g_ref, gd_ref, gup_ref, gb_ref, gn_ref, s0_ref,
                o_ref, so_ref, st_scr, b_scr, k_scr, *, C, SB):
    c = pl.program_id(2)
    nsb = C // SB

    @pl.when(c == 0)
    def _():
        st_scr[...] = s0_ref[...].T

    q = q_ref[...] * (HEAD_DIM ** -0.5)
    k = k_ref[...]
    v = v_ref[...]
    z = jnp.dot(gd_ref[...].astype(BF16), gup_ref[...], preferred_element_type=F32) + gb_ref[...]
    b = _cumsum_rows(_log_sigmoid(z) * (1.0 / GLA_TAU))
    b_scr[...] = b
    k_scr[...] = k

    row1 = lax.broadcasted_iota(jnp.int32, (C, 1), 0)
    blk1 = row1 // SB
    rin1 = row1 - blk1 * SB
    rowc = lax.broadcasted_iota(jnp.int32, (C, C), 0)
    lanec = lax.broadcasted_iota(jnp.int32, (C, C), 1)
    blkc = rowc // SB

    def rows_of(ref, off):
        parts = [jnp.broadcast_to(ref[sb * SB + off:sb * SB + off + 1, :], (SB, HEAD_DIM)) for sb in range(nsb)]
        return parts[0] if nsb == 1 else jnp.concatenate(parts, axis=0)

    A = jnp.zeros((C, C), F32)
    for j in range(SB):
        bj = rows_of(b_scr, j)
        kj = rows_of(k_scr, j)
        e = jnp.exp(jnp.where(rin1 >= j, b - bj, -jnp.inf))
        col = jnp.sum(q * kj * e, axis=-1, keepdims=True)
        A = jnp.where(lanec == blkc * SB + j, col, A)

    if nsb > 1:
        r_own = rows_of(b_scr, SB - 1)
        ktil = k * jnp.exp(r_own - b)
        for sbj in range(nsb - 1):
            rj = b_scr[(sbj + 1) * SB - 1:(sbj + 1) * SB, :]
            qt = q * jnp.exp(jnp.where(blk1 > sbj, b - rj, 0.0))
            kt = jnp.where(blk1 == sbj, ktil, 0.0)
            aj = _dot_nt(qt.astype(BF16), kt.astype(BF16))
            A = A + jnp.where(blkc > sbj, aj, 0.0)

    st = st_scr[...]
    bl = b[C - 1:C, :]
    k2 = k * jnp.exp(bl - b)
    if C >= 16:
        o = jnp.dot(A.astype(BF16), v.astype(BF16), preferred_element_type=F32)
        kv = _dot_tn(v.astype(BF16), k2.astype(BF16))
    else:
        o = A[:, 0:1] * v[0:1, :]
        for s_ in range(1, C):
            o += A[:, s_:s_ + 1] * v[s_:s_ + 1, :]
        zpad = jnp.zeros((16 - C, HEAD_DIM), F32)
        kv = _dot_tn(jnp.concatenate([v, zpad], axis=0).astype(BF16), jnp.concatenate([k2, zpad], axis=0).astype(BF16))
    o += _dot_nt((q * jnp.exp(b)).astype(BF16), st.astype(BF16))
    st_new = st * jnp.exp(bl) + kv
    st_scr[...] = st_new

    @pl.when(c == pl.num_programs(2) - 1)
    def _():
        so_ref[...] = st_new.T

    on = o * lax.rsqrt(jnp.mean(o * o, axis=-1, keepdims=True) + RMS_EPS) * gn_ref[...]
    gg = g_ref[...]
    o_ref[...] = (on * (gg * jax.nn.sigmoid(gg))).astype(BF16)


def _gla(proj, lay, B, T, gup_pad, gate_b, gla_norm, s0):
    GB = lay["GB"]
    H = GB
    if T % GLA_CHUNK == 0:
        C, SB = GLA_CHUNK, GLA_SUB
    else:
        C, SB = T, T
    nc = T // C

    def col(base):
        return pl.BlockSpec((C, LANES), lambda b, h, c: (b * nc + c, base + h))

    mix, s_out = pl.pallas_call(
        functools.partial(_gla_kernel, C=C, SB=SB),
        grid=(B, H, nc),
        in_specs=[col(0), col(GB), col(2 * GB), col(3 * GB),
                  pl.BlockSpec((C, LANES), lambda b, h, c: (b * nc + c, lay["SMALL_A"])),
                  pl.BlockSpec((LANES, LANES), lambda b, h, c: (0, h)),
                  pl.BlockSpec((1, LANES), lambda b, h, c: (0, h)),
                  pl.BlockSpec((1, LANES), lambda b, h, c: (0, h)),
                  pl.BlockSpec((None, None, HEAD_DIM, HEAD_DIM), lambda b, h, c: (b, h, 0, 0))],
        out_specs=[pl.BlockSpec((C, LANES), lambda b, h, c: (b * nc + c, h)),
                   pl.BlockSpec((None, None, HEAD_DIM, HEAD_DIM), lambda b, h, c: (b, h, 0, 0))],
        out_shape=[jax.ShapeDtypeStruct((B * T, H * HEAD_DIM), BF16),
                   jax.ShapeDtypeStruct((B, H, HEAD_DIM, HEAD_DIM), F32)],
        scratch_shapes=[pltpu.VMEM((HEAD_DIM, HEAD_DIM), F32),
                        pltpu.VMEM((C, HEAD_DIM), F32),
                        pltpu.VMEM((C, HEAD_DIM), F32)],
        compiler_params=_cp(("parallel", "parallel", "arbitrary")),
        name="gla",
    )(proj, proj, proj, proj, proj, gup_pad, gate_b.reshape(1, -1), gla_norm.reshape(1, -1), s0)
    return mix, s_out


def _foxcum_kernel(x_ref, fb_ref, lf_ref, cum_ref, carry):
    @pl.when(pl.program_id(1) == 0)
    def _():
        carry[...] = jnp.zeros_like(carry)

    lf = _log_sigmoid(x_ref[...] + fb_ref[...])
    cs = _cumsum_rows(lf) + carry[...]
    lf_ref[...] = lf
    cum_ref[...] = cs
    n = cs.shape[0]
    carry[...] = cs[n - 1:n, :]


def _fox_cum(proj, lay, B, T, fb_row):
    tt = _pick(T, 256, 8)
    nt = T // tt
    spec = pl.BlockSpec((tt, LANES), lambda b, t: (b * nt + t, 0))
    return pl.pallas_call(
        _foxcum_kernel,
        grid=(B, nt),
        in_specs=[pl.BlockSpec((tt, LANES), lambda b, t: (b * nt + t, lay["SMALL_A"])),
                  pl.BlockSpec((1, LANES), lambda b, t: (0, 0))],
        out_specs=[spec, spec],
        out_shape=[jax.ShapeDtypeStruct((B * T, LANES), F32)] * 2,
        scratch_shapes=[pltpu.VMEM((1, LANES), F32)],
        compiler_params=_cp(("parallel", "arbitrary")),
        name="fox_cum",
    )(proj, fb_row)


def _fox_prompt_kernel(q_ref, k_ref, v_ref, cq_ref, ck_ref, o_ref, *, tq):
    i = pl.program_id(2)
    T = k_ref.shape[0]
    q = (q_ref[...] * (HEAD_DIM ** -0.5)).astype(BF16)
    s = _dot_nt(q, k_ref[...].astype(BF16))
    s = s + cq_ref[...] - ck_ref[...]
    row = i * tq + lax.broadcasted_iota(jnp.int32, (tq, T), 0)
    colk = lax.broadcasted_iota(jnp.int32, (tq, T), 1)
    s = jnp.where(colk <= row, s, -jnp.inf)
    m = jnp.max(s, axis=-1, keepdims=True)
    p = jnp.exp(s - m)
    l = jnp.sum(p, axis=-1, keepdims=True)
    o = jnp.dot(p.astype(BF16), v_ref[...].astype(BF16), preferred_element_type=F32)
    o_ref[...] = (o / l).astype(BF16)


def _fox_prompt(proj, lay, B, T, cum_col, cum_row):
    GB = lay["GB"]
    H = GB
    tq = _pick(T, 256, 8)
    nq = T // tq
    return pl.pallas_call(
        functools.partial(_fox_prompt_kernel, tq=tq),
        grid=(B, H, nq),
        in_specs=[pl.BlockSpec((tq, LANES), lambda b, h, i: (b * nq + i, 4 * GB + h)),
                  pl.BlockSpec((T, LANES), lambda b, h, i: (b, 5 * GB + h)),
                  pl.BlockSpec((T, LANES), lambda b, h, i: (b, 6 * GB + h)),
                  pl.BlockSpec((None, None, tq, 1), lambda b, h, i: (b, h, i, 0)),
                  pl.BlockSpec((None, None, 1, T), lambda b, h, i: (b, h, 0, 0))],
        out_specs=pl.BlockSpec((tq, LANES), lambda b, h, i: (b * nq + i, h)),
        out_shape=jax.ShapeDtypeStruct((B * T, H * HEAD_DIM), BF16),
        compiler_params=_cp(("parallel", "parallel", "parallel")),
        name="fox_prompt",
    )(proj, proj, proj, cum_col, cum_row)


def _rope_kernel(q_ref, k_ref, cos_ref, sin_ref, qo_ref, ko_ref):
    cos = cos_ref[...]
    sin = sin_ref[...]
    q = q_ref[...]
    k = k_ref[...]
    qo_ref[...] = q * cos + pltpu.roll(q, HEAD_DIM // 2, axis=1) * sin
    ko_ref[...] = k * cos + pltpu.roll(k, HEAD_DIM // 2, axis=1) * sin


def _rope(proj, lay, B, T, pos0):
    GB = lay["GB"]
    H = GB
    half = HEAD_DIM // 2
    inv = 1.0 / (ROPE_THETA ** (jnp.arange(0, HEAD_DIM, 2, dtype=F32) / HEAD_DIM))
    ang = (pos0 + jnp.arange(T, dtype=jnp.int32)).astype(F32)[:, None] * inv[None, :]
    cos = jnp.concatenate([jnp.cos(ang), jnp.cos(ang)], axis=-1)
    sin = jnp.concatenate([-jnp.sin(ang), jnp.sin(ang)], axis=-1)
    assert cos.shape == (T, 2 * half)
    tt = _pick(T, 512, 8)
    nt = T // tt
    ospec = pl.BlockSpec((tt, LANES), lambda b, t, h: (b * nt + t, h))
    tspec = pl.BlockSpec((tt, LANES), lambda b, t, h: (t, 0))
    return pl.pallas_call(
        _rope_kernel,
        grid=(B, nt, H),
        in_specs=[pl.BlockSpec((tt, LANES), lambda b, t, h: (b * nt + t, 7 * GB + h)),
                  pl.BlockSpec((tt, LANES), lambda b, t, h: (b * nt + t, 8 * GB + h)),
                  tspec, tspec],
        out_specs=[ospec, ospec],
        out_shape=[jax.ShapeDtypeStruct((B * T, H * HEAD_DIM), F32)] * 2,
        compiler_params=_cp(("parallel", "parallel", "parallel")),
        name="rope",
    )(proj, proj, cos, sin)


def _top_lanes(g, avail, n_pick):
    lane = lax.broadcasted_iota(jnp.int32, g.shape, g.ndim - 1)
    sel = jnp.zeros(g.shape, dtype=jnp.bool_)
    for _ in range(n_pick):
        cur = jnp.where(avail, g, -jnp.inf)
        m = jnp.max(cur, axis=-1, keepdims=True)
        first = jnp.min(jnp.where(avail & (cur == m), lane, 1 << 30), axis=-1, keepdims=True)
        pick = avail & (lane == first)
        sel = sel | pick
        avail = avail & jnp.logical_not(pick)
    return sel


def _moba_prompt_kernel(q_ref, k_ref, v_ref, o_ref, *, tq):
    i = pl.program_id(2)
    T = k_ref.shape[0]
    NB = T // MOBA_BLOCK
    q = q_ref[...]
    k = k_ref[...]
    kmean = jnp.mean(k.reshape(NB, MOBA_BLOCK, HEAD_DIM), axis=1)
    if NB < LANES:
        kmean = jnp.concatenate([kmean, jnp.zeros((LANES - NB, HEAD_DIM), F32)], axis=0)
    g = _dot_nt(q, kmean, precision=HIGHEST)
    row1 = i * tq + lax.broadcasted_iota(jnp.int32, (tq, 1), 0)
    own1 = row1 // MOBA_BLOCK
    blk = lax.broadcasted_iota(jnp.int32, (tq, LANES), 1)
    sel = _top_lanes(g, blk < own1, MOBA_TOPK)
    expand = (lax.broadcasted_iota(jnp.int32, (LANES, T), 0)
              == lax.broadcasted_iota(jnp.int32, (LANES, T), 1) // MOBA_BLOCK).astype(BF16)
    selk = jnp.dot(sel.astype(BF16), expand, preferred_element_type=F32) > 0.5
    colk = lax.broadcasted_iota(jnp.int32, (tq, T), 1)
    ok = selk | ((colk // MOBA_BLOCK == own1) & (colk <= row1))
    s = _dot_nt((q * (HEAD_DIM ** -0.5)).astype(BF16), k.astype(BF16))
    s = jnp.where(ok, s, -jnp.inf)
    m = jnp.max(s, axis=-1, keepdims=True)
    p = jnp.exp(s - m)
    l = jnp.sum(p, axis=-1, keepdims=True)
    o = jnp.dot(p.astype(BF16), v_ref[...].astype(BF16), preferred_element_type=F32)
    o_ref[...] = (o / l).astype(BF16)


def _moba_prompt(mq, mk, proj, lay, B, T):
    GB = lay["GB"]
    H = GB
    assert T % MOBA_BLOCK == 0 and T // MOBA_BLOCK <= LANES
    tq = _pick(T, 256, 8)
    nq = T // tq
    return pl.pallas_call(
        functools.partial(_moba_prompt_kernel, tq=tq),
        grid=(B, H, nq),
        in_specs=[pl.BlockSpec((tq, LANES), lambda b, h, i: (b * nq + i, h)),
                  pl.BlockSpec((T, LANES), lambda b, h, i: (b, h)),
                  pl.BlockSpec((T, LANES), lambda b, h, i: (b, 9 * GB + h))],
        out_specs=pl.BlockSpec((tq, LANES), lambda b, h, i: (b * nq + i, h)),
        out_shape=jax.ShapeDtypeStruct((B * T, H * HEAD_DIM), BF16),
        compiler_params=_cp(("parallel", "parallel", "parallel")),
        name="moba_prompt",
    )(mq, mk, proj)


def _new_token_columns(h, q, kn_ref, vn_ref, bias_col, bias_row_at, m, l, acc):
    T = q.shape[0]
    trow = lax.broadcasted_iota(jnp.int32, (T, 1), 0)
    cols = []
    for s_ in range(T):
        kn = kn_ref[s_:s_ + 1, h * HEAD_DIM:(h + 1) * HEAD_DIM]
        c = jnp.sum(q * kn, axis=-1, keepdims=True) + bias_col - bias_row_at(s_)
        cols.append(jnp.where(trow >= s_, c, NEG))
    m_new = m
    for c in cols:
        m_new = jnp.maximum(m_new, c)
    alpha = jnp.exp(m - m_new)
    l = alpha * l
    acc = alpha * acc
    for s_, c in enumerate(cols):
        p = jnp.where(trow >= s_, jnp.exp(c - m_new), 0.0)
        l = l + p
        acc = acc + p * vn_ref[s_:s_ + 1, h * HEAD_DIM:(h + 1) * HEAD_DIM]
    return l, acc


def _fox_sample_kernel(pt_ref, *refs, H, PP):
    del pt_ref
    k_refs = refs[0:PP]
    v_refs = refs[PP:2 * PP]
    lf_refs = refs[2 * PP:3 * PP]
    qn_ref, kn_ref, vn_ref, cc_ref, cr_ref, o_ref, m_scr, l_scr, acc_scr, carry = refs[3 * PP:]
    j = pl.program_id(1)
    T = qn_ref.shape[0]
    PAGE = k_refs[0].shape[0]

    @pl.when(j == 0)
    def _():
        m_scr[...] = jnp.full(m_scr.shape, NEG, F32)
        l_scr[...] = jnp.zeros_like(l_scr)
        acc_scr[...] = jnp.zeros_like(acc_scr)
        carry[...] = jnp.zeros_like(carry)

    lane = lax.broadcasted_iota(jnp.int32, (H, PAGE), 1)
    biases = []
    run = carry[...]
    for i in range(PP):
        x = lf_refs[i][...]
        y = x
        s_ = 1
        while s_ < PAGE:
            y = y + jnp.where(lane + s_ < PAGE, pltpu.roll(y, PAGE - s_, axis=1), 0.0)
            s_ *= 2
        biases.append(y - x + run)
        run = run + jnp.broadcast_to(y[:, 0:1], (H, PAGE))
    carry[...] = run

    for h in range(H):
        q = qn_ref[:, h * HEAD_DIM:(h + 1) * HEAD_DIM] * (HEAD_DIM ** -0.5)
        qb = q.astype(BF16)
        cq = cc_ref[:, h:h + 1]
        s = jnp.concatenate(
            [_dot_nt(qb, k_refs[i][:, h, :].astype(BF16)) + biases[i][h:h + 1, :] for i in range(PP)],
            axis=1) + cq
        m_old = m_scr[h][:, 0:1]
        m_new = jnp.maximum(m_old, jnp.max(s, axis=-1, keepdims=True))
        alpha = jnp.exp(m_old - m_new)
        p = jnp.exp(s - m_new)
        l_new = alpha * l_scr[h][:, 0:1] + jnp.sum(p, axis=-1, keepdims=True)
        acc = alpha * acc_scr[h]
        for i in range(PP):
            acc += jnp.dot(p[:, i * PAGE:(i + 1) * PAGE].astype(BF16), v_refs[i][:, h, :].astype(BF16),
                           preferred_element_type=F32)
        m_scr[h] = jnp.broadcast_to(m_new, (T, LANES))
        l_scr[h] = jnp.broadcast_to(l_new, (T, LANES))
        acc_scr[h] = acc

    @pl.when(j == pl.num_programs(1) - 1)
    def _():
        for h in range(H):
            q = qn_ref[:, h * HEAD_DIM:(h + 1) * HEAD_DIM] * (HEAD_DIM ** -0.5)
            l, acc = _new_token_columns(
                h, q, kn_ref, vn_ref, cc_ref[:, h:h + 1], lambda s_: cr_ref[h:h + 1, s_:s_ + 1],
                m_scr[h][:, 0:1], l_scr[h][:, 0:1], acc_scr[h])
            o_ref[:, h * HEAD_DIM:(h + 1) * HEAD_DIM] = (acc / l).astype(BF16)


def _fox_sample(proj, lay, DB, T, layer, cache_k, cache_v, cache_lf_t, page_table, cum_cols, cum_rows):
    GB = lay["GB"]
    H = GB
    G = H * HEAD_DIM
    PAGE = cache_k.shape[2]
    n_pages = page_table.shape[1]
    PP = PAGES_PER_STEP if n_pages % PAGES_PER_STEP == 0 else 1
    NS = n_pages // PP

    def kv_spec(i):
        return pl.BlockSpec((None, None, PAGE, H, HEAD_DIM),
                            lambda b, j, pt: (layer, pt[b, n_pages - 1 - (j * PP + i)], 0, 0, 0))

    def lf_spec(i):
        return pl.BlockSpec((None, None, H, PAGE),
                            lambda b, j, pt: (layer, pt[b, n_pages - 1 - (j * PP + i)], 0, 0))

    def new_spec(idx):
        return pl.BlockSpec((T, G), lambda b, j, pt: (b, idx))

    grid_spec = pltpu.PrefetchScalarGridSpec(
        num_scalar_prefetch=1,
        grid=(DB, NS),
        in_specs=([kv_spec(i) for i in range(PP)] + [kv_spec(i) for i in range(PP)]
                  + [lf_spec(i) for i in range(PP)]
                  + [new_spec(4), new_spec(5), new_spec(6),
                     pl.BlockSpec((None, T, H), lambda b, j, pt: (b, 0, 0)),
                     pl.BlockSpec((None, H, T), lambda b, j, pt: (b, 0, 0))]),
        out_specs=pl.BlockSpec((T, G), lambda b, j, pt: (b, 0)),
        scratch_shapes=[pltpu.VMEM((H, T, LANES), F32), pltpu.VMEM((H, T, LANES), F32),
                        pltpu.VMEM((H, T, HEAD_DIM), F32), pltpu.VMEM((H, PAGE), F32)],
    )
    return pl.pallas_call(
        functools.partial(_fox_sample_kernel, H=H, PP=PP),
        grid_spec=grid_spec,
        out_shape=jax.ShapeDtypeStruct((DB * T, G), BF16),
        compiler_params=_cp(("parallel", "arbitrary")),
        name="fox_sample",
    )(page_table, *([cache_k] * PP), *([cache_v] * PP), *([cache_lf_t] * PP),
      proj, proj, proj, cum_cols, cum_rows)


def _moba_sel_kernel(pt_ref, *refs, H, PP, PPB):
    del pt_ref
    k_refs = refs[0:PP]
    q_ref, sel_ref, g_scr = refs[PP:]
    j = pl.program_id(1)
    T = q_ref.shape[0]
    PAGE = k_refs[0].shape[0]

    @pl.when(j == 0)
    def _():
        g_scr[...] = jnp.zeros_like(g_scr)

    lane = lax.broadcasted_iota(jnp.int32, (H, LANES), 1)
    for bi in range(PP // PPB):
        ks = jnp.sum(k_refs[bi * PPB][...], axis=0)
        for r in range(1, PPB):
            ks = ks + jnp.sum(k_refs[bi * PPB + r][...], axis=0)
        kmean = ks * (1.0 / (PPB * PAGE))
        blk = j * (PP // PPB) + bi
        for t in range(T):
            gcol = jnp.sum(q_ref[t] * kmean, axis=-1, keepdims=True)
            g_scr[t] = jnp.where(lane == blk, gcol, g_scr[t])

    @pl.when(j == pl.num_programs(1) - 1)
    def _():
        n_blk = pl.num_programs(1) * (PP // PPB)
        for t in range(T):
            sel = _top_lanes(g_scr[t], lane < n_blk, MOBA_TOPK)
            sel_ref[t] = sel.astype(F32)


def _moba_sel(mq3, layer, cache_k, page_table):
    DB, T, H, _ = mq3.shape
    PAGE = cache_k.shape[2]
    n_pages = page_table.shape[1]
    PPB = MOBA_BLOCK // PAGE
    PP = PAGES_PER_STEP if n_pages % PAGES_PER_STEP == 0 else PPB
    assert MOBA_BLOCK % PAGE == 0 and PP % PPB == 0 and n_pages % PP == 0 and n_pages // PPB <= LANES
    NS = n_pages // PP

    def k_spec(i):
        return pl.BlockSpec((None, None, PAGE, H, HEAD_DIM),
                            lambda b, j, pt: (layer, pt[b, j * PP + i], 0, 0, 0))

    grid_spec = pltpu.PrefetchScalarGridSpec(
        num_scalar_prefetch=1,
        grid=(DB, NS),
        in_specs=[k_spec(i) for i in range(PP)]
                 + [pl.BlockSpec((None, T, H, HEAD_DIM), lambda b, j, pt: (b, 0, 0, 0))],
        out_specs=pl.BlockSpec((None, T, H, LANES), lambda b, j, pt: (b, 0, 0, 0)),
        scratch_shapes=[pltpu.VMEM((T, H, LANES), F32)],
    )
    return pl.pallas_call(
        functools.partial(_moba_sel_kernel, H=H, PP=PP, PPB=PPB),
        grid_spec=grid_spec,
        out_shape=jax.ShapeDtypeStruct((DB, T, H, LANES), F32),
        compiler_params=_cp(("parallel", "arbitrary")),
        name="moba_sel",
    )(page_table, *([cache_k] * PP), mq3)


def _moba_sample_kernel(pt_ref, *refs, H, PP, PPB):
    del pt_ref
    k_refs = refs[0:PP]
    v_refs = refs[PP:2 * PP]
    qn_ref, kn_ref, vn_ref, sel_ref, o_ref, m_scr, l_scr, acc_scr = refs[2 * PP:]
    j = pl.program_id(1)
    T = qn_ref.shape[0]
    PAGE = k_refs[0].shape[0]
    W = PP * PAGE

    @pl.when(j == 0)
    def _():
        m_scr[...] = jnp.full(m_scr.shape, NEG, F32)
        l_scr[...] = jnp.zeros_like(l_scr)
        acc_scr[...] = jnp.zeros_like(acc_scr)

    expand = (lax.broadcasted_iota(jnp.int32, (LANES, W), 0)
              == j * (PP // PPB) + lax.broadcasted_iota(jnp.int32, (LANES, W), 1) // (PPB * PAGE)).astype(BF16)
    for h in range(H):
        qb = (qn_ref[:, h * HEAD_DIM:(h + 1) * HEAD_DIM] * (HEAD_DIM ** -0.5)).astype(BF16)
        ok = jnp.dot(sel_ref[:, h, :].astype(BF16), expand, preferred_element_type=F32) > 0.5
        s = jnp.concatenate([_dot_nt(qb, k_refs[i][:, h, :].astype(BF16)) for i in range(PP)], axis=1)
        s = jnp.where(ok, s, NEG)
        m_old = m_scr[h][:, 0:1]
        m_new = jnp.maximum(m_old, jnp.max(s, axis=-1, keepdims=True))
        alpha = jnp.exp(m_old - m_new)
        p = jnp.where(ok, jnp.exp(s - m_new), 0.0)
        l_new = alpha * l_scr[h][:, 0:1] + jnp.sum(p, axis=-1, keepdims=True)
        acc = alpha * acc_scr[h]
        for i in range(PP):
            acc += jnp.dot(p[:, i * PAGE:(i + 1) * PAGE].astype(BF16), v_refs[i][:, h, :].astype(BF16),
                           preferred_element_type=F32)
        m_scr[h] = jnp.broadcast_to(m_new, (T, LANES))
        l_scr[h] = jnp.broadcast_to(l_new, (T, LANES))
        acc_scr[h] = acc

    @pl.when(j == pl.num_programs(1) - 1)
    def _():
        zero = jnp.zeros((1, 1), F32)
        for h in range(H):
            q = qn_ref[:, h * HEAD_DIM:(h + 1) * HEAD_DIM] * (HEAD_DIM ** -0.5)
            l, acc = _new_token_columns(h, q, kn_ref, vn_ref, zero, lambda s_: zero,
                                        m_scr[h][:, 0:1], l_scr[h][:, 0:1], acc_scr[h])
            o_ref[:, h * HEAD_DIM:(h + 1) * HEAD_DIM] = (acc / l).astype(BF16)


def _moba_sample(mq, mk, proj, lay, DB, T, layer, cache_k, cache_v, page_table, sel):
    GB = lay["GB"]
    H = GB
    G = H * HEAD_DIM
    PAGE = cache_k.shape[2]
    n_pages = page_table.shape[1]
    PPB = MOBA_BLOCK // PAGE
    PP = PAGES_PER_STEP if n_pages % PAGES_PER_STEP == 0 else PPB
    NS = n_pages // PP
    assert (n_pages * PAGE) % MOBA_BLOCK == 0 and T <= MOBA_BLOCK

    def kv_spec(i):
        return pl.BlockSpec((None, None, PAGE, H, HEAD_DIM),
                            lambda b, j, pt: (layer, pt[b, j * PP + i], 0, 0, 0))

    grid_spec = pltpu.PrefetchScalarGridSpec(
        num_scalar_prefetch=1,
        grid=(DB, NS),
        in_specs=([kv_spec(i) for i in range(PP)] + [kv_spec(i) for i in range(PP)]
                  + [pl.BlockSpec((T, G), lambda b, j, pt: (b, 0)),
                     pl.BlockSpec((T, G), lambda b, j, pt: (b, 0)),
                     pl.BlockSpec((T, G), lambda b, j, pt: (b, 9)),
                     pl.BlockSpec((None, T, H, LANES), lambda b, j, pt: (b, 0, 0, 0))]),
        out_specs=pl.BlockSpec((T, G), lambda b, j, pt: (b, 0)),
        scratch_shapes=[pltpu.VMEM((H, T, LANES), F32), pltpu.VMEM((H, T, LANES), F32),
                        pltpu.VMEM((H, T, HEAD_DIM), F32)],
    )
    return pl.pallas_call(
        functools.partial(_moba_sample_kernel, H=H, PP=PP, PPB=PPB),
        grid_spec=grid_spec,
        out_shape=jax.ShapeDtypeStruct((DB * T, G), BF16),
        compiler_params=_cp(("parallel", "arbitrary")),
        name="moba_sample",
    )(page_table, *([cache_k] * PP), *([cache_v] * PP), mq, mk, proj, sel)


def _seg_sum(x):
    bd = (lax.broadcasted_iota(jnp.int32, (LANES, LANES), 0) // RWKV_HEAD
          == lax.broadcasted_iota(jnp.int32, (LANES, LANES), 1) // RWKV_HEAD).astype(F32)
    outs = [jnp.dot(x[:, c * LANES:(c + 1) * LANES], bd, precision=HIGHEST, preferred_element_type=F32)
            for c in range(x.shape[1] // LANES)]
    return outs[0] if len(outs) == 1 else jnp.concatenate(outs, axis=1)


def _rwkv_prep_kernel(r_ref, k_ref, v_ref, d_ref, rp_ref, kp_ref, vp_ref, dp_ref,
                      s0m_ref, s0d_ref, mum_ref, mud_ref,
                      wup_ref, aup_ref, gup_ref, w0_ref, a0_ref, kk_ref, ka_ref, rk_ref,
                      ro, wo, ko, vo, kko, kkao, go, bo):
    t = pl.program_id(1)
    tt, G = r_ref.shape
    first = t == 0

    def mixed(cur_ref, prev_ref, s0, mu):
        cur = cur_ref[...]
        n = cur.shape[0]
        prow = jnp.where(first, s0, prev_ref[prev_ref.shape[0] - 1:, :])
        row = lax.broadcasted_iota(jnp.int32, cur.shape, 0)
        prev = jnp.where(row == 0, prow, pltpu.roll(cur, 1, axis=0)) if n > 1 else prow
        return cur + (prev - cur) * mu

    r = mixed(r_ref, rp_ref, s0m_ref[:, 0:G], mum_ref[:, 0:G])
    kr = mixed(k_ref, kp_ref, s0m_ref[:, G:2 * G], mum_ref[:, G:2 * G])
    vr = mixed(v_ref, vp_ref, s0m_ref[:, 2 * G:3 * G], mum_ref[:, 2 * G:3 * G])
    d = mixed(d_ref, dp_ref, s0d_ref[...], mud_ref[...])

    dwa = d[:, 0:LANES]
    tw = jnp.dot(jnp.tanh(dwa).astype(BF16), wup_ref[...], preferred_element_type=F32)
    w_raw = -_softplus(-(w0_ref[...] + tw)) - 0.5
    w = jnp.exp(-jnp.exp(w_raw))
    a = jax.nn.sigmoid(a0_ref[...] + jnp.dot(dwa.astype(BF16), aup_ref[...], preferred_element_type=F32))
    g = jnp.dot(jax.nn.sigmoid(d[:, LANES:]).astype(BF16), gup_ref[...], preferred_element_type=F32)
    kk = kr * kk_ref[...]
    kk = kk / jnp.maximum(jnp.sqrt(_seg_sum(kk * kk)), 1e-12)
    k2 = kr * (1.0 + (a - 1.0) * ka_ref[...])
    ro[...] = r
    wo[...] = w
    ko[...] = k2
    vo[...] = vr
    kko[...] = kk
    kkao[...] = kk * a
    go[...] = g
    bo[...] = _seg_sum(r * k2 * rk_ref[...]) * vr


def _rwkv_prep(proj, lay, B, T, shift_main, shift_down, mu_main, mu_down, wts):
    GB = lay["GB"]
    G = GB * LANES
    tt = _pick(T, 256, 8)
    nt = T // tt
    dcol = lay["DOWN"]
    assert (dcol * LANES) % (3 * LANES) == 0
    dblk = dcol // 3

    def cur(idx, w):
        return pl.BlockSpec((tt, w), lambda b, t: (b * nt + t, idx))

    def prev(idx, w):
        return pl.BlockSpec((8, w), lambda b, t: (jnp.maximum((b * T + t * tt) // 8 - 1, 0), idx))

    full = lambda a: pl.BlockSpec(a.shape, lambda b, t: (0,) * a.ndim)
    ospec = pl.BlockSpec((tt, G), lambda b, t: (b * nt + t, 0))
    outs = pl.pallas_call(
        _rwkv_prep_kernel,
        grid=(B, nt),
        in_specs=[cur(10, G), cur(11, G), cur(12, G), cur(dblk, 3 * LANES),
                  prev(10, G), prev(11, G), prev(12, G), prev(dblk, 3 * LANES),
                  pl.BlockSpec((None, 1, 3 * G), lambda b, t: (b, 0, 0)),
                  pl.BlockSpec((None, 1, 3 * LANES), lambda b, t: (b, 0, 0)),
                  full(mu_main), full(mu_down)] + [full(a) for a in wts],
        out_specs=[ospec] * 8,
        out_shape=[jax.ShapeDtypeStruct((B * T, G), F32)] * 8,
        compiler_params=_cp(("parallel", "arbitrary")),
        name="rwkv_prep",
    )(proj, proj, proj, proj, proj, proj, proj, proj, shift_main, shift_down, mu_main, mu_down, *wts)
    return outs


def _rwkv_scan_kernel(r_ref, w_ref, k_ref, kk_ref, kka_ref, v_ref, s0_ref, y_ref, so_ref, s_scr):
    i = pl.program_id(0)
    tt, K, L = r_ref.shape

    @pl.when(i == 0)
    def _():
        s_scr[...] = s0_ref[...]

    def row(ref, t, kidx):
        return ref[t, pl.ds(kidx, 1), :]

    def token(t, carry):
        vv = v_ref[t]
        rt = r_ref[t]
        kr_dot = jnp.sum(k_ref[t] * rt, axis=0, keepdims=True)
        kar_dot = jnp.sum(kka_ref[t] * rt, axis=0, keepdims=True)
        sa = jnp.zeros(vv.shape, F32)
        ya = jnp.zeros(vv.shape, F32)
        for kidx in range(K):
            s = s_scr[kidx]
            sa = sa - s * row(kk_ref, t, kidx)
            ya = ya + s * (row(w_ref, t, kidx) * row(r_ref, t, kidx))
        y_ref[t] = ya + sa * kar_dot + vv * kr_dot
        for kidx in range(K):
            s_scr[kidx] = (s_scr[kidx] * row(w_ref, t, kidx) + sa * row(kka_ref, t, kidx)
                           + vv * row(k_ref, t, kidx))
        return carry

    lax.fori_loop(0, tt, token, 0)

    @pl.when(i == pl.num_programs(0) - 1)
    def _():
        so_ref[...] = s_scr[...]


def _rwkv_scan(r, w, k, kk, kka, v, s0):
    T, K, L = r.shape
    VP = v.shape[1]
    tt = _pick(T, 32, 1)
    kspec = pl.BlockSpec((tt, K, L), lambda i: (i, 0, 0))
    vspec = pl.BlockSpec((tt, VP, L), lambda i: (i, 0, 0))
    sspec = pl.BlockSpec((K, VP, L), lambda i: (0, 0, 0))
    return pl.pallas_call(
        _rwkv_scan_kernel,
        grid=(T // tt,),
        in_specs=[kspec] * 5 + [vspec, sspec],
        out_specs=[vspec, sspec],
        out_shape=[jax.ShapeDtypeStruct((T, VP, L), F32), jax.ShapeDtypeStruct((K, VP, L), F32)],
        scratch_shapes=[pltpu.VMEM((K, VP, L), F32)],
        compiler_params=_cp(("arbitrary",)),
        name="rwkv_scan",
    )(r, w, k, kk, kka, v, s0)


def _rwkv_post_kernel(y_ref, b_ref, g_ref, lw_ref, lb_ref, o_ref):
    y = y_ref[...]
    mu = _seg_sum(y) * (1.0 / RWKV_HEAD)
    yc = y - mu
    var = _seg_sum(yc * yc) * (1.0 / RWKV_HEAD)
    yn = yc * lax.rsqrt(var + RWKV_LN_EPS) * lw_ref[...] + lb_ref[...]
    o_ref[...] = ((yn + b_ref[...]) * g_ref[...]).astype(BF16)


def _rwkv_post(y, bonus, g, ln_w, ln_b):
    M, G = y.shape
    tm = _pick(M, 512, 8)
    spec = pl.BlockSpec((tm, G), lambda i: (i, 0))
    rspec = pl.BlockSpec((1, G), lambda i: (0, 0))
    return pl.pallas_call(
        _rwkv_post_kernel,
        grid=(M // tm,),
        in_specs=[spec, spec, spec, rspec, rspec],
        out_specs=spec,
        out_shape=jax.ShapeDtypeStruct((M, G), BF16),
        compiler_params=_cp(("parallel",)),
        name="rwkv_post",
    )(y, bonus, g, ln_w.reshape(1, G), ln_b.reshape(1, G))


def _rwkv(proj, lay, B, T, shift0, state0, p, l):
    GB = lay["GB"]
    G = GB * LANES
    RH = G // RWKV_HEAD
    N = RWKV_HEAD
    chains = B * RH
    assert LANES % chains == 0 or chains % LANES == 0
    VH = max(1, LANES // chains)
    VP = N // VH
    rw = p["rwkv_w_up"][l].shape[0]
    ra = p["rwkv_a_up"][l].shape[0]
    rg = p["rwkv_g_up"][l].shape[0]
    assert rw + ra == LANES and rg <= 2 * LANES

    def padrows(a, top, total):
        return jnp.pad(a, ((top, total - top - a.shape[0]), (0, 0))).astype(BF16)

    def pad_down(a):
        return jnp.pad(a, [(0, 0)] * (a.ndim - 1) + [(0, 3 * LANES - a.shape[-1])])

    wts = (padrows(p["rwkv_w_up"][l], 0, LANES), padrows(p["rwkv_a_up"][l], rw, LANES),
           padrows(p["rwkv_g_up"][l], 0, 2 * LANES),
           p["rwkv_w0"][l].reshape(1, G), p["rwkv_a0"][l].reshape(1, G),
           p["rwkv_k_k"][l].reshape(1, G), p["rwkv_k_a"][l].reshape(1, G), p["rwkv_r_k"][l].reshape(1, G))
    mu = p["rwkv_mu"][l]
    r, w, k, v, kk, kka, g, bonus = _rwkv_prep(
        proj, lay, B, T,
        shift0[:, None, :3 * G], pad_down(shift0[:, None, 3 * G:]),
        mu[None, :3 * G], pad_down(mu[None, 3 * G:]), wts)

    def to_k(x):
        x = x.reshape(B, T, RH, N).transpose(1, 3, 0, 2).reshape(T, N, chains)
        return jnp.tile(x, (1, 1, VH))

    v_t = v.reshape(B, T, RH, VH, VP).transpose(1, 4, 3, 0, 2).reshape(T, VP, VH * chains)
    s0_t = state0.reshape(B, RH, VH, VP, N).transpose(4, 3, 2, 0, 1).reshape(N, VP, VH * chains)
    y_t, s_t = _rwkv_scan(to_k(r), to_k(w), to_k(k), to_k(kk), to_k(kka), v_t, s0_t)
    y = y_t.reshape(T, VP, VH, B, RH).transpose(3, 0, 4, 2, 1).reshape(B * T, G)
    s_new = s_t.reshape(N, VP, VH, B, RH).transpose(3, 4, 2, 1, 0).reshape(B, RH, N, N)
    mix = _rwkv_post(y, bonus, g, p["rwkv_ln_w"][l], p["rwkv_ln_b"][l])
    return mix, s_new


def _peer_topk_kernel(q_ref, keys_ref, e_ref, g_ref, v1_scr, i1_scr, v2_scr, i2_scr, c_scr, ci_scr, t_scr, te_scr):
    NK = keys_ref.shape[2]
    KH = keys_ref.shape[3]
    tn = q_ref.shape[0]
    TK = PEER_TOPK
    big = jnp.int32(1 << 30)

    def top_rows(s, iota, vals_scr, idx_scr, payload=None):
        for kk in range(TK):
            m = jnp.max(s, axis=0, keepdims=True)
            first = jnp.min(jnp.where(s == m, iota, big), axis=0, keepdims=True)
            hit = iota == first
            vals_scr[kk:kk + 1, :] = m
            if payload is None:
                idx_scr[kk:kk + 1, :] = first
            else:
                idx_scr[kk:kk + 1, :] = jnp.max(jnp.where(hit, payload, -1), axis=0, keepdims=True)
            s = jnp.where(hit, -jnp.inf, s)

    iota_k = lax.broadcasted_iota(jnp.int32, (NK, tn), 0)
    iota_c = lax.broadcasted_iota(jnp.int32, (TK * TK, tn), 0)
    for h in range(PEER_HEADS):
        for half, (vs, is_) in enumerate(((v1_scr, i1_scr), (v2_scr, i2_scr))):
            c0 = (h * 2 + half) * KH
            s = _dot_nt(keys_ref[h, half], q_ref[:, c0:c0 + KH], precision=HIGHEST)
            top_rows(s, iota_k, vs, is_)
        v2 = v2_scr[...]
        i2 = i2_scr[...]
        for a in range(TK):
            c_scr[a * TK:(a + 1) * TK, :] = v1_scr[a:a + 1, :] + v2
            ci_scr[a * TK:(a + 1) * TK, :] = i1_scr[a:a + 1, :] * NK + i2
        top_rows(c_scr[...], iota_c, t_scr, te_scr, payload=ci_scr[...])
        top = t_scr[...]
        ex = jnp.exp(top - top[0:1, :])
        g_ref[h * TK:(h + 1) * TK, :] = ex / jnp.sum(ex, axis=0, keepdims=True)
        e_ref[h * TK:(h + 1) * TK, :] = te_scr[...]


def _peer_topk(qp, keys):
    M = qp.shape[0]
    NK = keys.shape[2]
    tn = _pick(M, 256)
    J = PEER_HEADS * PEER_TOPK
    TK = PEER_TOPK
    return pl.pallas_call(
        _peer_topk_kernel,
        grid=(M // tn,),
        in_specs=[pl.BlockSpec((tn, qp.shape[1]), lambda i: (i, 0)),
                  pl.BlockSpec(keys.shape, lambda i: (0, 0, 0, 0))],
        out_specs=[pl.BlockSpec((J, tn), lambda i: (0, i))] * 2,
        out_shape=[jax.ShapeDtypeStruct((J, M), jnp.int32), jax.ShapeDtypeStruct((J, M), F32)],
        scratch_shapes=[pltpu.VMEM((TK, tn), F32), pltpu.VMEM((TK, tn), jnp.int32),
                        pltpu.VMEM((TK, tn), F32), pltpu.VMEM((TK, tn), jnp.int32),
                        pltpu.VMEM((TK * TK, tn), F32), pltpu.VMEM((TK * TK, tn), jnp.int32),
                        pltpu.VMEM((TK, tn), F32), pltpu.VMEM((TK, tn), jnp.int32)],
        compiler_params=_cp(("parallel",)),
        name="peer_topk",
    )(qp, keys)


def _peer_w_kernel(e_ref, g_ref, w_ref, *, NK):
    tw, J = e_ref.shape
    iota = lax.broadcasted_iota(jnp.int32, (NK, J), 0)
    shift = NK.bit_length() - 1
    assert 1 << shift == NK

    def token(n, carry):
        e = e_ref[pl.ds(n, 1), :]
        g = g_ref[pl.ds(n, 1), :]
        a = e >> shift
        b = e & (NK - 1)
        at = jnp.where(iota == a, g, 0.0).astype(BF16)
        bt = (iota == b).astype(BF16)
        w_ref[n] = _dot_nt(at, bt).astype(BF16)
        return carry

    lax.fori_loop(0, tw, token, 0)


def _peer_w(eidx, gates, NK):
    M, J = eidx.shape
    tw = _pick(M, 64, 8)
    return pl.pallas_call(
        functools.partial(_peer_w_kernel, NK=NK),
        grid=(M // tw,),
        in_specs=[pl.BlockSpec((tw, J), lambda i: (i, 0))] * 2,
        out_specs=pl.BlockSpec((tw, NK, NK), lambda i: (i, 0, 0)),
        out_shape=jax.ShapeDtypeStruct((M, NK, NK), BF16),
        compiler_params=_cp(("parallel",)),
        name="peer_w",
    )(eidx, gates)


def _peer_dense_kernel(h_ref, u_ref, w_ref, v_ref, x_ref, ga_ref, o_ref):
    e = pl.program_id(2)

    @pl.when(e == 0)
    def _():
        o_ref[...] = jnp.zeros_like(o_ref)

    a = _dot_nt(h_ref[...], u_ref[...])
    act = 0.5 * a * (1.0 + lax.erf(a * (2.0 ** -0.5)))
    z = (act * w_ref[...].astype(F32)).astype(BF16)
    o_ref[...] += jnp.dot(z, v_ref[...], preferred_element_type=F32).reshape(o_ref.shape)

    @pl.when(e == pl.num_programs(2) - 1)
    def _():
        o_ref[...] = x_ref[...] + ga_ref[...] * o_ref[...]


def _peer_dense(x3, h2, wflat, u, v, mod3, ga_idx):
    B, T, D = x3.shape
    E = u.shape[0]
    bb, tt = _row_tiles(B, T)
    te = _pick(E, 512)
    nt = T // tt
    once = pl.Buffered(1)
    return pl.pallas_call(
        _peer_dense_kernel,
        grid=(B // bb, nt, E // te),
        in_specs=[pl.BlockSpec((bb * tt, D), lambda b, t, e: (b * nt + t, 0), pipeline_mode=once),
                  pl.BlockSpec((te, D), lambda b, t, e: (e, 0)),
                  pl.BlockSpec((bb * tt, te), lambda b, t, e: (b * nt + t, e)),
                  pl.BlockSpec((te, D), lambda b, t, e: (e, 0)),
                  pl.BlockSpec((bb, tt, D), lambda b, t, e: (b, t, 0), pipeline_mode=once),
                  pl.BlockSpec((bb, 1, D), lambda b, t, e: (b, 0, ga_idx))],
        out_specs=pl.BlockSpec((bb, tt, D), lambda b, t, e: (b, t, 0)),
        out_shape=jax.ShapeDtypeStruct((B, T, D), F32),
        compiler_params=_cp(("parallel", "parallel", "arbitrary")),
        name="peer_dense",
    )(h2, u, wflat, v, x3, mod3)


def _layout(G):
    GB = G // LANES
    return {"GB": GB, "SMALL_A": 13 * GB, "DOWN": 13 * GB + 1, "NP": (13 * GB + 4) * LANES}


def _pack_w_in(w_in, G, gate_rank, n_fox_heads, n_down):
    gla, fox = 0, 4 * G + gate_rank
    moba = fox + 3 * G + n_fox_heads
    rw = moba + 3 * G
    D = w_in.shape[0]
    zeros = lambda n: jnp.zeros((D, n), w_in.dtype)
    parts = [w_in[:, gla:gla + 4 * G], w_in[:, fox:fox + 3 * G], w_in[:, moba:moba + 3 * G], w_in[:, rw:rw + 3 * G],
             w_in[:, 4 * G:4 * G + gate_rank], w_in[:, fox + 3 * G:fox + 3 * G + n_fox_heads],
             zeros(LANES - gate_rank - n_fox_heads),
             w_in[:, rw + 3 * G:rw + 3 * G + n_down], zeros(3 * LANES - n_down)]
    return jnp.concatenate(parts, axis=1).astype(BF16)


def _layer(x3, mod3, l, p, wl, past, page_table):
    B, T, D = x3.shape
    G = D // 4
    lay = _layout(G)
    GB = lay["GB"]
    H = GB
    gate_rank = p["gla_gate_up"].shape[1]
    M = B * T

    proj = _norm_mod_matmul(x3, mod3, 1, 0, p["norm_mix"][l], wl["w_in"], emit_h=False, name="in_proj")

    mix_gla, gla_s = _gla(proj, lay, B, T, wl["gup"], p["gla_gate_b"][l], p["gla_norm"][l], past["gla"])

    lf, cum = _fox_cum(proj, lay, B, T, wl["fb_row"])
    lf8 = lf[:, gate_rank:gate_rank + H].reshape(B, T, H)
    cum8 = cum[:, gate_rank:gate_rank + H].reshape(B, T, H)
    if page_table is None:
        mix_fox = _fox_prompt(proj, lay, B, T, cum8.transpose(0, 2, 1)[..., None], cum8.transpose(0, 2, 1)[:, :, None, :])
    else:
        mix_fox = _fox_sample(proj, lay, B, T, l, past["fox_k"], past["fox_v"], past["fox_lf_t"], page_table,
                              cum8, cum8.transpose(0, 2, 1))

    pos0 = 0 if page_table is None else page_table.shape[1] * past["moba_k"].shape[2]
    mq, mk = _rope(proj, lay, B, T, pos0)
    if page_table is None:
        mix_moba = _moba_prompt(mq, mk, proj, lay, B, T)
    else:
        sel = _moba_sel(mq.reshape(B, T, H, HEAD_DIM), l, past["moba_k"], page_table)
        mix_moba = _moba_sample(mq, mk, proj, lay, B, T, l, past["moba_k"], past["moba_v"], page_table, sel)

    mix_rwkv, rwkv_s = _rwkv(proj, lay, B, T, past["shift"], past["rwkv"], p, l)

    x3 = _out_proj(x3, (mix_gla, mix_fox, mix_moba, mix_rwkv), wl["w_out"], mod3, 2)

    qp, h2 = _norm_mod_matmul(x3, mod3, 4, 3, p["norm_ffn"][l], wl["wq"], emit_h=True, name="peer_q")
    NK = p["peer_keys"].shape[3]
    eidx_t, gate_t = _peer_topk(qp, p["peer_keys"][l])
    w3 = _peer_w(eidx_t.T, gate_t.T, NK)
    x3 = _peer_dense(x3, h2, w3.reshape(M, NK * NK), wl["u"], wl["v"], mod3, 5)

    def heads(c0):
        return proj[:, c0 * LANES:(c0 + GB) * LANES].reshape(B, T, H, HEAD_DIM)

    last = proj.reshape(B, T, -1)[:, T - 1]
    n_down = p["rwkv_mu"].shape[1] - 3 * G
    shift = jnp.concatenate([last[:, 10 * G:13 * G], last[:, lay["DOWN"] * LANES:lay["DOWN"] * LANES + n_down]], axis=-1)
    new = (heads(5 * GB), heads(6 * GB), lf8, mk.reshape(B, T, H, HEAD_DIM), heads(9 * GB), gla_s, rwkv_s, shift)
    return x3, new


def kernel(x_prompt, x_sample, cache_fox_k, cache_fox_v, cache_fox_logf, cache_moba_k, cache_moba_v, state_gla, state_rwkv, state_rwkv_shift, page_table, c_prompt, c_sample, w_mod, b_mod, norm_mix, norm_ffn, w_in, gla_gate_up, gla_gate_b, gla_norm, fox_forget_b, rwkv_mu, rwkv_w0, rwkv_w_up, rwkv_a0, rwkv_a_up, rwkv_g_up, rwkv_k_k, rwkv_k_a, rwkv_r_k, rwkv_ln_w, rwkv_ln_b, w_out, peer_wq, peer_keys, peer_u, peer_v, final_norm):
    p = dict(norm_mix=norm_mix, norm_ffn=norm_ffn, gla_gate_up=gla_gate_up, gla_gate_b=gla_gate_b,
             gla_norm=gla_norm, rwkv_mu=rwkv_mu, rwkv_w0=rwkv_w0, rwkv_w_up=rwkv_w_up, rwkv_a0=rwkv_a0,
             rwkv_a_up=rwkv_a_up, rwkv_g_up=rwkv_g_up, rwkv_k_k=rwkv_k_k, rwkv_k_a=rwkv_k_a, rwkv_r_k=rwkv_r_k,
             rwkv_ln_w=rwkv_ln_w, rwkv_ln_b=rwkv_ln_b, peer_keys=peer_keys)
    L, D, _ = w_mod.shape
    G = D // 4
    B, T, _ = x_prompt.shape
    DB, DT, _ = x_sample.shape
    H = G // HEAD_DIM
    RH = G // RWKV_HEAD
    gate_rank = gla_gate_up.shape[1]
    n_down = rwkv_mu.shape[1] - 3 * G

    R = -(-(DB + B) // 8) * 8
    c_all = jnp.concatenate([c_sample, c_prompt, jnp.zeros((R - DB - B, D), F32)], axis=0)
    mod = _adaln_mod(c_all, w_mod, b_mod)

    fox_lf_t = cache_fox_logf.transpose(0, 1, 3, 2)
    xp, xs = x_prompt, x_sample
    news_p, news_s = [], []
    for l in range(L):
        wl = {
            "w_in": _pack_w_in(w_in[l], G, gate_rank, H, n_down),
            "w_out": w_out[l].astype(BF16),
            "wq": peer_wq[l].astype(BF16),
            "u": peer_u[l].astype(BF16),
            "v": peer_v[l].astype(BF16),
            "gup": jnp.pad(gla_gate_up[l], ((0, LANES - gate_rank), (0, 0))).astype(BF16),
            "fb_row": jnp.pad(fox_forget_b[l], (gate_rank, LANES - gate_rank - H)).reshape(1, LANES),
        }
        past_p = {"gla": jnp.zeros((B, H, HEAD_DIM, HEAD_DIM), F32),
                  "rwkv": jnp.zeros((B, RH, RWKV_HEAD, RWKV_HEAD), F32),
                  "shift": jnp.zeros((B, 3 * G + n_down), F32)}
        past_s = {"gla": state_gla[l], "rwkv": state_rwkv[l], "shift": state_rwkv_shift[l],
                  "fox_k": cache_fox_k, "fox_v": cache_fox_v, "fox_lf_t": fox_lf_t,
                  "moba_k": cache_moba_k, "moba_v": cache_moba_v}
        xp, new_p = _layer(xp, mod[l, DB:DB + B, None, :], l, p, wl, past_p, None)
        xs, new_s = _layer(xs, mod[l, 0:DB, None, :], l, p, wl, past_s, page_table)
        news_p.append(new_p)
        news_s.append(new_s)
    y_prompt = _final_norm(xp, final_norm)
    y_sample = _final_norm(xs, final_norm)
    stack = lambda news: [jnp.stack([n[i] for n in news]) for i in range(8)]
    return (y_prompt, y_sample, *stack(news_p), *stack(news_s))
```

```python
import functools

import jax
import jax.numpy as jnp
from jax import lax
from jax.experimental import pallas as pl
from jax.experimental.pallas import tpu as pltpu

F32 = jnp.float32
BF16 = jnp.bfloat16
HIGHEST = lax.Precision.HIGHEST

LANES = 128
HEAD_DIM = 128
RWKV_HEAD = 64
N_MOD = 6
RMS_EPS = 1e-6
GLA_TAU = 16.0
GLA_CHUNK = 64
GLA_SUB = 16
MOBA_BLOCK = 256
MOBA_TOPK = 3
ROPE_THETA = 10000.0
RWKV_LN_EPS = 64e-5
PEER_HEADS = 8
PEER_TOPK = 16
NEG = -1e30
VMEM_LIMIT = 52 * 1024 * 1024
PAGES_PER_STEP = 8


def _cp(sem, vmem=VMEM_LIMIT):
    return pltpu.CompilerParams(dimension_semantics=sem, vmem_limit_bytes=vmem)


def _pick(n, pref, mult=LANES):
    best = None
    d = mult
    while d <= min(n, pref):
        if n % d == 0:
            best = d
        d += mult
    return best if best is not None else n


def _row_tiles(B, T):
    if T >= 256:
        return 1, _pick(T, 512, 8)
    return B, T


def _log_sigmoid(z):
    return jnp.minimum(z, 0.0) - jnp.log(1.0 + jnp.exp(-jnp.abs(z)))


def _softplus(z):
    return jnp.maximum(z, 0.0) + jnp.log(1.0 + jnp.exp(-jnp.abs(z)))


def _cumsum_rows(x):
    n = x.shape[0]
    row = lax.broadcasted_iota(jnp.int32, x.shape, 0)
    s = 1
    while s < n:
        x = x + jnp.where(row >= s, pltpu.roll(x, s, axis=0), 0.0)
        s *= 2
    return x


def _dot_nt(a, b, **kw):
    return lax.dot_general(a, b, (((1,), (1,)), ((), ())), preferred_element_type=F32, **kw)


def _dot_tn(a, b):
    return lax.dot_general(a, b, (((0,), (0,)), ((), ())), preferred_element_type=F32)


def _mod_kernel(c_ref, w_ref, b_ref, o_ref):
    c = c_ref[...]
    ca = (c * jax.nn.sigmoid(c)).astype(BF16)
    o_ref[...] = jnp.dot(ca, w_ref[...].astype(BF16), preferred_element_type=F32) + b_ref[...]


def _adaln_mod(c_all, w_mod, b_mod):
    L, D, N = w_mod.shape
    R = c_all.shape[0]
    tn = _pick(N, 512)
    return pl.pallas_call(
        _mod_kernel,
        grid=(L, N // tn),
        in_specs=[pl.BlockSpec((R, D), lambda l, n: (0, 0)),
                  pl.BlockSpec((None, D, tn), lambda l, n: (l, 0, n)),
                  pl.BlockSpec((None, 1, tn), lambda l, n: (l, 0, n))],
        out_specs=pl.BlockSpec((None, R, tn), lambda l, n: (l, 0, n)),
        out_shape=jax.ShapeDtypeStruct((L, R, N), F32),
        compiler_params=_cp(("parallel", "parallel")),
        name="adaln_mod",
    )(c_all, w_mod, b_mod.reshape(L, 1, N))


def _nm_kernel(x_ref, sc_ref, sh_ref, gain_ref, w_ref, o_ref, *rest, emit_h):
    if emit_h:
        h_out_ref, h_scr = rest
    else:
        (h_scr,) = rest

    @pl.when(pl.program_id(2) == 0)
    def _():
        x = x_ref[...]
        ms = jnp.mean(x * x, axis=-1, keepdims=True)
        y = x * lax.rsqrt(ms + RMS_EPS) * gain_ref[...]
        h = y * (1.0 + sc_ref[...]) + sh_ref[...]
        hb = h.reshape(h_scr.shape).astype(BF16)
        h_scr[...] = hb
        if emit_h:
            h_out_ref[...] = hb

    o_ref[...] = jnp.dot(h_scr[...], w_ref[...], preferred_element_type=F32)


def _norm_mod_matmul(x3, mod3, sc_idx, sh_idx, gain, w, *, emit_h, name):
    B, T, D = x3.shape
    N = w.shape[1]
    bb, tt = _row_tiles(B, T)
    tn = _pick(N, 512)
    M = B * T
    nt = T // tt
    out_shape = [jax.ShapeDtypeStruct((M, N), F32)]
    out_specs = [pl.BlockSpec((bb * tt, tn), lambda b, t, n: (b * nt + t, n))]
    if emit_h:
        out_shape.append(jax.ShapeDtypeStruct((M, D), BF16))
        out_specs.append(pl.BlockSpec((bb * tt, D), lambda b, t, n: (b * nt + t, 0)))
    res = pl.pallas_call(
        functools.partial(_nm_kernel, emit_h=emit_h),
        grid=(B // bb, nt, N // tn),
        in_specs=[pl.BlockSpec((bb, tt, D), lambda b, t, n: (b, t, 0)),
                  pl.BlockSpec((bb, 1, D), lambda b, t, n: (b, 0, sc_idx)),
                  pl.BlockSpec((bb, 1, D), lambda b, t, n: (b, 0, sh_idx)),
                  pl.BlockSpec((1, D), lambda b, t, n: (0, 0)),
                  pl.BlockSpec((D, tn), lambda b, t, n: (0, n))],
        out_specs=out_specs,
        out_shape=out_shape,
        scratch_shapes=[pltpu.VMEM((bb * tt, D), BF16)],
        compiler_params=_cp(("parallel", "parallel", "arbitrary")),
        name=name,
    )(x3, mod3, mod3, gain.reshape(1, D), w)
    return res if emit_h else res[0]


def _op_kernel(x_ref, m0, m1, m2, m3, w_ref, ga_ref, o_ref):
    G = m0.shape[1]
    acc = jnp.dot(m0[...], w_ref[0:G, :], preferred_element_type=F32)
    acc += jnp.dot(m1[...], w_ref[G:2 * G, :], preferred_element_type=F32)
    acc += jnp.dot(m2[...], w_ref[2 * G:3 * G, :], preferred_element_type=F32)
    acc += jnp.dot(m3[...], w_ref[3 * G:4 * G, :], preferred_element_type=F32)
    o_ref[...] = x_ref[...] + ga_ref[...] * acc.reshape(x_ref.shape)


def _out_proj(x3, mixes, w_out, mod3, ga_idx):
    B, T, D = x3.shape
    G = mixes[0].shape[1]
    bb, tt = _row_tiles(B, T)
    tn = _pick(D, 1024)
    nt = T // tt
    nd = D // tn
    mspec = pl.BlockSpec((bb * tt, G), lambda b, t, n: (b * nt + t, 0))
    return pl.pallas_call(
        _op_kernel,
        grid=(B // bb, nt, nd),
        in_specs=[pl.BlockSpec((bb, tt, tn), lambda b, t, n: (b, t, n)),
                  mspec, mspec, mspec, mspec,
                  pl.BlockSpec((D, tn), lambda b, t, n: (0, n)),
                  pl.BlockSpec((bb, 1, tn), lambda b, t, n: (b, 0, ga_idx * nd + n))],
        out_specs=pl.BlockSpec((bb, tt, tn), lambda b, t, n: (b, t, n)),
        out_shape=jax.ShapeDtypeStruct((B, T, D), F32),
        compiler_params=_cp(("parallel", "parallel", "parallel")),
        name="out_proj",
    )(x3, *mixes, w_out, mod3)


def _fnorm_kernel(x_ref, g_ref, o_ref):
    x = x_ref[...]
    ms = jnp.mean(x * x, axis=-1, keepdims=True)
    o_ref[...] = x * lax.rsqrt(ms + RMS_EPS) * g_ref[...]


def _final_norm(x3, gain):
    B, T, D = x3.shape
    M = B * T
    tm = _pick(M, 512, 8)
    y = pl.pallas_call(
        _fnorm_kernel,
        grid=(M // tm,),
        in_specs=[pl.BlockSpec((tm, D), lambda i: (i, 0)), pl.BlockSpec((1, D), lambda i: (0, 0))],
        out_specs=pl.BlockSpec((tm, D), lambda i: (i, 0)),
        out_shape=jax.ShapeDtypeStruct((M, D), F32),
        compiler_params=_cp(("parallel",)),
        name="final_norm",
    )(x3.reshape(M, D), gain.reshape(1, D))
    return y.reshape(B, T, D)


def _gla_head(q, k, v, gg, z, gn, st_scr, b_scr, k_scr, C, SB):
    nsb = C // SB
    q = q * (HEAD_DIM ** -0.5)
    b = _cumsum_rows(_log_sigmoid(z) * (1.0 / GLA_TAU))
    b_scr[...] = b
    k_scr[...] = k

    row1 = lax.broadcasted_iota(jnp.int32, (C, 1), 0)
    blk1 = row1 // SB
    rin1 = row1 - blk1 * SB
    rowc = lax.broadcasted_iota(jnp.int32, (C, C), 0)
    lanec = lax.broadcasted_iota(jnp.int32, (C, C), 1)
    blkc = rowc // SB

    def rows_of(ref, off):
        parts = [jnp.broadcast_to(ref[sb * SB + off:sb * SB + off + 1, :], (SB, HEAD_DIM)) for sb in range(nsb)]
        return parts[0] if nsb == 1 else jnp.concatenate(parts, axis=0)

    A = jnp.zeros((C, C), F32)
    for j in range(SB):
        bj = rows_of(b_scr, j)
        kj = rows_of(k_scr, j)
        e = jnp.exp(jnp.where(rin1 >= j, b - bj, -jnp.inf))
        col = jnp.sum(q * kj * e, axis=-1, keepdims=True)
        A = jnp.where(lanec == blkc * SB + j, col, A)

    if nsb > 1:
        r_own = rows_of(b_scr, SB - 1)
        ktil = k * jnp.exp(r_own - b)
        for sbj in range(nsb - 1):
            rj = b_scr[(sbj + 1) * SB - 1:(sbj + 1) * SB, :]
            qt = q * jnp.exp(jnp.where(blk1 > sbj, b - rj, 0.0))
            kt = jnp.where(blk1 == sbj, ktil, 0.0)
            aj = _dot_nt(qt.astype(BF16), kt.astype(BF16))
            A = A + jnp.where(blkc > sbj, aj, 0.0)

    st = st_scr[...]
    bl = b[C - 1:C, :]
    k2 = k * jnp.exp(bl - b)
    if C >= 16:
        o = jnp.dot(A.astype(BF16), v.astype(BF16), preferred_element_type=F32)
        kv = _dot_tn(v.astype(BF16), k2.astype(BF16))
    else:
        o = A[:, 0:1] * v[0:1, :]
        for s_ in range(1, C):
            o += A[:, s_:s_ + 1] * v[s_:s_ + 1, :]
        zpad = jnp.zeros((16 - C, HEAD_DIM), F32)
        kv = _dot_tn(jnp.concatenate([v, zpad], axis=0).astype(BF16), jnp.concatenate([k2, zpad], axis=0).astype(BF16))
    o += _dot_nt((q * jnp.exp(b)).astype(BF16), st.astype(BF16))
    st_scr[...] = st * jnp.exp(bl) + kv
    on = o * lax.rsqrt(jnp.mean(o * o, axis=-1, keepdims=True) + RMS_EPS) * gn
    return on * (gg * jax.nn.sigmoid(gg))


def _gla_kernel(q_ref, k_ref, v_ref, g_ref, gd_ref, gup_ref, gb_ref, gn_ref, s0_ref,
                o_ref, so_ref, st_scr, b_scr, k_scr, *, C, SB, HB):
    c = pl.program_id(2)

    @pl.when(c == 0)
    def _():
        for i in range(HB):
            st_scr[i] = s0_ref[i].T

    z_all = jnp.dot(gd_ref[...].astype(BF16), gup_ref[...], preferred_element_type=F32) + gb_ref[...]
    for i in range(HB):
        sl = slice(i * HEAD_DIM, (i + 1) * HEAD_DIM)
        out = _gla_head(q_ref[:, sl], k_ref[:, sl], v_ref[:, sl], g_ref[:, sl], z_all[:, sl], gn_ref[:, sl],
                        st_scr.at[i], b_scr.at[i], k_scr.at[i], C, SB)
        o_ref[:, sl] = out.astype(BF16)

    @pl.when(c == pl.num_programs(2) - 1)
    def _():
        for i in range(HB):
            so_ref[i] = st_scr[i].T


def _gla(proj, lay, B, T, gup_pad, gate_b, gla_norm, s0):
    GB = lay["GB"]
    H = GB
    if T % GLA_CHUNK == 0:
        C, SB = GLA_CHUNK, GLA_SUB
    else:
        C, SB = T, T
    nc = T // C
    HB = 4 if H % 4 == 0 else (2 if H % 2 == 0 else 1)
    W = HB * LANES

    def col(base):
        return pl.BlockSpec((C, W), lambda b, h, c: (b * nc + c, base // HB + h))

    mix, s_out = pl.pallas_call(
        functools.partial(_gla_kernel, C=C, SB=SB, HB=HB),
        grid=(B, H // HB, nc),
        in_specs=[col(0), col(GB), col(2 * GB), col(3 * GB),
                  pl.BlockSpec((C, LANES), lambda b, h, c: (b * nc + c, lay["SMALL_A"])),
                  pl.BlockSpec((LANES, W), lambda b, h, c: (0, h)),
                  pl.BlockSpec((1, W), lambda b, h, c: (0, h)),
                  pl.BlockSpec((1, W), lambda b, h, c: (0, h)),
                  pl.BlockSpec((None, HB, HEAD_DIM, HEAD_DIM), lambda b, h, c: (b, h, 0, 0))],
        out_specs=[pl.BlockSpec((C, W), lambda b, h, c: (b * nc + c, h)),
                   pl.BlockSpec((None, HB, HEAD_DIM, HEAD_DIM), lambda b, h, c: (b, h, 0, 0))],
        out_shape=[jax.ShapeDtypeStruct((B * T, H * HEAD_DIM), BF16),
                   jax.ShapeDtypeStruct((B, H, HEAD_DIM, HEAD_DIM), F32)],
        scratch_shapes=[pltpu.VMEM((HB, HEAD_DIM, HEAD_DIM), F32),
                        pltpu.VMEM((HB, C, HEAD_DIM), F32),
                        pltpu.VMEM((HB, C, HEAD_DIM), F32)],
        compiler_params=_cp(("parallel", "parallel", "arbitrary")),
        name="gla",
    )(proj, proj, proj, proj, proj, gup_pad, gate_b.reshape(1, -1), gla_norm.reshape(1, -1), s0)
    return mix, s_out


def _foxcum_kernel(x_ref, fb_ref, lf_ref, cum_ref, carry):
    @pl.when(pl.program_id(1) == 0)
    def _():
        carry[...] = jnp.zeros_like(carry)

    lf = _log_sigmoid(x_ref[...] + fb_ref[...])
    cs = _cumsum_rows(lf) + carry[...]
    lf_ref[...] = lf
    cum_ref[...] = cs
    n = cs.shape[0]
    carry[...] = cs[n - 1:n, :]


def _fox_cum(proj, lay, B, T, fb_row):
    tt = _pick(T, 256, 8)
    nt = T // tt
    spec = pl.BlockSpec((tt, LANES), lambda b, t: (b * nt + t, 0))
    return pl.pallas_call(
        _foxcum_kernel,
        grid=(B, nt),
        in_specs=[pl.BlockSpec((tt, LANES), lambda b, t: (b * nt + t, lay["SMALL_A"])),
                  pl.BlockSpec((1, LANES), lambda b, t: (0, 0))],
        out_specs=[spec, spec],
        out_shape=[jax.ShapeDtypeStruct((B * T, LANES), F32)] * 2,
        scratch_shapes=[pltpu.VMEM((1, LANES), F32)],
        compiler_params=_cp(("parallel", "arbitrary")),
        name="fox_cum",
    )(proj, fb_row)


def _fox_prompt_kernel(q_ref, k_ref, v_ref, cq_ref, ck_ref, o_ref, *, tq):
    i = pl.program_id(2)
    T = k_ref.shape[0]
    q = (q_ref[...] * (HEAD_DIM ** -0.5)).astype(BF16)
    s = _dot_nt(q, k_ref[...].astype(BF16))
    s = s + cq_ref[...] - ck_ref[...]
    row = i * tq + lax.broadcasted_iota(jnp.int32, (tq, T), 0)
    colk = lax.broadcasted_iota(jnp.int32, (tq, T), 1)
    s = jnp.where(colk <= row, s, -jnp.inf)
    m = jnp.max(s, axis=-1, keepdims=True)
    p = jnp.exp(s - m)
    l = jnp.sum(p, axis=-1, keepdims=True)
    o = jnp.dot(p.astype(BF16), v_ref[...].astype(BF16), preferred_element_type=F32)
    o_ref[...] = (o / l).astype(BF16)


def _fox_prompt(proj, lay, B, T, cum_col, cum_row):
    GB = lay["GB"]
    H = GB
    tq = _pick(T, 256, 8)
    nq = T // tq
    return pl.pallas_call(
        functools.partial(_fox_prompt_kernel, tq=tq),
        grid=(B, H, nq),
        in_specs=[pl.BlockSpec((tq, LANES), lambda b, h, i: (b * nq + i, 4 * GB + h)),
                  pl.BlockSpec((T, LANES), lambda b, h, i: (b, 5 * GB + h)),
                  pl.BlockSpec((T, LANES), lambda b, h, i: (b, 6 * GB + h)),
                  pl.BlockSpec((None, None, tq, 1), lambda b, h, i: (b, h, i, 0)),
                  pl.BlockSpec((None, None, 1, T), lambda b, h, i: (b, h, 0, 0))],
        out_specs=pl.BlockSpec((tq, LANES), lambda b, h, i: (b * nq + i, h)),
        out_shape=jax.ShapeDtypeStruct((B * T, H * HEAD_DIM), BF16),
        compiler_params=_cp(("parallel", "parallel", "parallel")),
        name="fox_prompt",
    )(proj, proj, proj, cum_col, cum_row)


def _rope_kernel(q_ref, k_ref, cos_ref, sin_ref, qo_ref, ko_ref):
    cos = cos_ref[...]
    sin = sin_ref[...]
    q = q_ref[...]
    k = k_ref[...]
    qo_ref[...] = q * cos + pltpu.roll(q, HEAD_DIM // 2, axis=1) * sin
    ko_ref[...] = k * cos + pltpu.roll(k, HEAD_DIM // 2, axis=1) * sin


def _rope(proj, lay, B, T, pos0):
    GB = lay["GB"]
    H = GB
    half = HEAD_DIM // 2
    inv = 1.0 / (ROPE_THETA ** (jnp.arange(0, HEAD_DIM, 2, dtype=F32) / HEAD_DIM))
    ang = (pos0 + jnp.arange(T, dtype=jnp.int32)).astype(F32)[:, None] * inv[None, :]
    cos = jnp.concatenate([jnp.cos(ang), jnp.cos(ang)], axis=-1)
    sin = jnp.concatenate([-jnp.sin(ang), jnp.sin(ang)], axis=-1)
    assert cos.shape == (T, 2 * half)
    tt = _pick(T, 512, 8)
    nt = T // tt
    ospec = pl.BlockSpec((tt, LANES), lambda b, t, h: (b * nt + t, h))
    tspec = pl.BlockSpec((tt, LANES), lambda b, t, h: (t, 0))
    return pl.pallas_call(
        _rope_kernel,
        grid=(B, nt, H),
        in_specs=[pl.BlockSpec((tt, LANES), lambda b, t, h: (b * nt + t, 7 * GB + h)),
                  pl.BlockSpec((tt, LANES), lambda b, t, h: (b * nt + t, 8 * GB + h)),
                  tspec, tspec],
        out_specs=[ospec, ospec],
        out_shape=[jax.ShapeDtypeStruct((B * T, H * HEAD_DIM), F32)] * 2,
        compiler_params=_cp(("parallel", "parallel", "parallel")),
        name="rope",
    )(proj, proj, cos, sin)


def _top_lanes(g, avail, n_pick):
    lane = lax.broadcasted_iota(jnp.int32, g.shape, g.ndim - 1)
    sel = jnp.zeros(g.shape, dtype=jnp.bool_)
    for _ in range(n_pick):
        cur = jnp.where(avail, g, -jnp.inf)
        m = jnp.max(cur, axis=-1, keepdims=True)
        first = jnp.min(jnp.where(avail & (cur == m), lane, 1 << 30), axis=-1, keepdims=True)
        pick = avail & (lane == first)
        sel = sel | pick
        avail = avail & jnp.logical_not(pick)
    return sel


def _moba_prompt_kernel(q_ref, k_ref, v_ref, o_ref, *, tq):
    i = pl.program_id(2)
    T = k_ref.shape[0]
    NB = T // MOBA_BLOCK
    q = q_ref[...]
    k = k_ref[...]
    kmean = jnp.mean(k.reshape(NB, MOBA_BLOCK, HEAD_DIM), axis=1)
    if NB < LANES:
        kmean = jnp.concatenate([kmean, jnp.zeros((LANES - NB, HEAD_DIM), F32)], axis=0)
    g = _dot_nt(q, kmean, precision=HIGHEST)
    row1 = i * tq + lax.broadcasted_iota(jnp.int32, (tq, 1), 0)
    own1 = row1 // MOBA_BLOCK
    blk = lax.broadcasted_iota(jnp.int32, (tq, LANES), 1)
    sel = _top_lanes(g, blk < own1, MOBA_TOPK)
    expand = (lax.broadcasted_iota(jnp.int32, (LANES, T), 0)
              == lax.broadcasted_iota(jnp.int32, (LANES, T), 1) // MOBA_BLOCK).astype(BF16)
    selk = jnp.dot(sel.astype(BF16), expand, preferred_element_type=F32) > 0.5
    colk = lax.broadcasted_iota(jnp.int32, (tq, T), 1)
    ok = selk | ((colk // MOBA_BLOCK == own1) & (colk <= row1))
    s = _dot_nt((q * (HEAD_DIM ** -0.5)).astype(BF16), k.astype(BF16))
    s = jnp.where(ok, s, -jnp.inf)
    m = jnp.max(s, axis=-1, keepdims=True)
    p = jnp.exp(s - m)
    l = jnp.sum(p, axis=-1, keepdims=True)
    o = jnp.dot(p.astype(BF16), v_ref[...].astype(BF16), preferred_element_type=F32)
    o_ref[...] = (o / l).astype(BF16)


def _moba_prompt(mq, mk, proj, lay, B, T):
    GB = lay["GB"]
    H = GB
    assert T % MOBA_BLOCK == 0 and T // MOBA_BLOCK <= LANES
    tq = _pick(T, 256, 8)
    nq = T // tq
    return pl.pallas_call(
        functools.partial(_moba_prompt_kernel, tq=tq),
        grid=(B, H, nq),
        in_specs=[pl.BlockSpec((tq, LANES), lambda b, h, i: (b * nq + i, h)),
                  pl.BlockSpec((T, LANES), lambda b, h, i: (b, h)),
                  pl.BlockSpec((T, LANES), lambda b, h, i: (b, 9 * GB + h))],
        out_specs=pl.BlockSpec((tq, LANES), lambda b, h, i: (b * nq + i, h)),
        out_shape=jax.ShapeDtypeStruct((B * T, H * HEAD_DIM), BF16),
        compiler_params=_cp(("parallel", "parallel", "parallel")),
        name="moba_prompt",
    )(mq, mk, proj)


def _head_keys(page_refs, h, H):
    PAGE = page_refs[0].shape[0] // H
    parts = [r[pl.ds(h, PAGE, stride=H), :].astype(BF16) for r in page_refs]
    return parts[0] if len(parts) == 1 else jnp.concatenate(parts, axis=0)


def _paged_rows(cache):
    L, NPOOL, PAGE, H, hd = cache.shape
    rows = cache.reshape(L * NPOOL * PAGE * H, hd)

    def spec(layer, page_of):
        return pl.BlockSpec((PAGE * H, hd), lambda b, j, pt: (layer * NPOOL + page_of(b, j, pt), 0))

    return rows, spec


def _paged_step(H, qn_ref, k_refs, v_refs, adjust, m_scr, l_scr, acc_scr):
    nblk = len(k_refs) * (k_refs[0].shape[0] // H) // LANES
    logits = []
    for h in range(H):
        qb = (qn_ref[:, h * HEAD_DIM:(h + 1) * HEAD_DIM] * (HEAD_DIM ** -0.5)).astype(BF16)
        s = _dot_nt(qb, _head_keys(k_refs, h, H))
        logits.append([adjust(h, i, s[:, i * LANES:(i + 1) * LANES]) for i in range(nblk)])
    probs, alphas = [], []
    for h in range(H):
        mx = logits[h][0][0]
        for sb, _ in logits[h][1:]:
            mx = jnp.maximum(mx, sb)
        m_old = m_scr[h]
        m_new = jnp.maximum(m_old, jnp.max(mx, axis=-1, keepdims=True))
        alpha = jnp.exp(m_old - m_new)
        pb = [jnp.exp(sb - m_new) if ok is None else jnp.where(ok, jnp.exp(sb - m_new), 0.0) for sb, ok in logits[h]]
        tot = pb[0]
        for x in pb[1:]:
            tot = tot + x
        m_scr[h] = m_new
        l_scr[h] = alpha * l_scr[h] + jnp.sum(tot, axis=-1, keepdims=True)
        probs.append((pb[0] if nblk == 1 else jnp.concatenate(pb, axis=1)).astype(BF16))
        alphas.append(alpha)
    for h in range(H):
        acc_scr[h] = alphas[h] * acc_scr[h] + jnp.dot(probs[h], _head_keys(v_refs, h, H), preferred_element_type=F32)


def _new_token_columns(h, q, kn_ref, vn_ref, bias_col, bias_row_at, m, l, acc):
    T = q.shape[0]
    trow = lax.broadcasted_iota(jnp.int32, (T, 1), 0)
    cols = []
    for s_ in range(T):
        kn = kn_ref[s_:s_ + 1, h * HEAD_DIM:(h + 1) * HEAD_DIM]
        c = jnp.sum(q * kn, axis=-1, keepdims=True) + bias_col - bias_row_at(s_)
        cols.append(jnp.where(trow >= s_, c, NEG))
    m_new = m
    for c in cols:
        m_new = jnp.maximum(m_new, c)
    alpha = jnp.exp(m - m_new)
    l = alpha * l
    acc = alpha * acc
    for s_, c in enumerate(cols):
        p = jnp.where(trow >= s_, jnp.exp(c - m_new), 0.0)
        l = l + p
        acc = acc + p * vn_ref[s_:s_ + 1, h * HEAD_DIM:(h + 1) * HEAD_DIM]
    return l, acc


def _fox_sample_kernel(pt_ref, *refs, H, PP):
    del pt_ref
    k_refs = refs[0:PP]
    v_refs = refs[PP:2 * PP]
    lf_refs = refs[2 * PP:3 * PP]
    qn_ref, kn_ref, vn_ref, cc_ref, cr_ref, o_ref, m_scr, l_scr, acc_scr, carry = refs[3 * PP:]
    j = pl.program_id(1)
    T = qn_ref.shape[0]
    PAGE = k_refs[0].shape[0] // H

    @pl.when(j == 0)
    def _():
        m_scr[...] = jnp.full(m_scr.shape, NEG, F32)
        l_scr[...] = jnp.zeros_like(l_scr)
        acc_scr[...] = jnp.zeros_like(acc_scr)
        carry[...] = jnp.zeros_like(carry)

    lane = lax.broadcasted_iota(jnp.int32, (H, PAGE), 1)
    biases = []
    run = carry[...]
    for i in range(PP):
        x = lf_refs[i][...]
        y = x
        s_ = 1
        while s_ < PAGE:
            y = y + jnp.where(lane + s_ < PAGE, pltpu.roll(y, PAGE - s_, axis=1), 0.0)
            s_ *= 2
        biases.append(y - x + run)
        run = run + jnp.broadcast_to(y[:, 0:1], (H, PAGE))
    carry[...] = run

    assert PAGE == LANES
    _paged_step(H, qn_ref, k_refs, v_refs,
                lambda h, i, sb: (sb + biases[i][h:h + 1, :] + cc_ref[h], None),
                m_scr, l_scr, acc_scr)

    @pl.when(j == pl.num_programs(1) - 1)
    def _():
        for h in range(H):
            q = qn_ref[:, h * HEAD_DIM:(h + 1) * HEAD_DIM] * (HEAD_DIM ** -0.5)
            l, acc = _new_token_columns(
                h, q, kn_ref, vn_ref, cc_ref[h], lambda s_: cr_ref[h:h + 1, s_:s_ + 1],
                m_scr[h], l_scr[h], acc_scr[h])
            o_ref[:, h * HEAD_DIM:(h + 1) * HEAD_DIM] = (acc / l).astype(BF16)


def _fox_sample(proj, lay, DB, T, layer, cache_k, cache_v, cache_lf_t, page_table, cum_cols, cum_rows):
    GB = lay["GB"]
    H = GB
    G = H * HEAD_DIM
    PAGE = cache_k.shape[2]
    n_pages = page_table.shape[1]
    PP = PAGES_PER_STEP if n_pages % PAGES_PER_STEP == 0 else 1
    NS = n_pages // PP
    k_rows, page_spec = _paged_rows(cache_k)
    v_rows, _ = _paged_rows(cache_v)

    def kv_spec(i):
        return page_spec(layer, lambda b, j, pt: pt[b, n_pages - 1 - (j * PP + i)])

    def lf_spec(i):
        return pl.BlockSpec((None, None, H, PAGE),
                            lambda b, j, pt: (layer, pt[b, n_pages - 1 - (j * PP + i)], 0, 0))

    def new_spec(idx):
        return pl.BlockSpec((T, G), lambda b, j, pt: (b, idx))

    grid_spec = pltpu.PrefetchScalarGridSpec(
        num_scalar_prefetch=1,
        grid=(DB, NS),
        in_specs=([kv_spec(i) for i in range(PP)] + [kv_spec(i) for i in range(PP)]
                  + [lf_spec(i) for i in range(PP)]
                  + [new_spec(4), new_spec(5), new_spec(6),
                     pl.BlockSpec((None, H, T, LANES), lambda b, j, pt: (b, 0, 0, 0)),
                     pl.BlockSpec((None, H, T), lambda b, j, pt: (b, 0, 0))]),
        out_specs=pl.BlockSpec((T, G), lambda b, j, pt: (b, 0)),
        scratch_shapes=[pltpu.VMEM((H, T, LANES), F32), pltpu.VMEM((H, T, LANES), F32),
                        pltpu.VMEM((H, T, HEAD_DIM), F32), pltpu.VMEM((H, PAGE), F32)],
    )
    return pl.pallas_call(
        functools.partial(_fox_sample_kernel, H=H, PP=PP),
        grid_spec=grid_spec,
        out_shape=jax.ShapeDtypeStruct((DB * T, G), BF16),
        compiler_params=_cp(("parallel", "arbitrary")),
        name="fox_sample",
    )(page_table, *([k_rows] * PP), *([v_rows] * PP), *([cache_lf_t] * PP),
      proj, proj, proj, cum_cols, cum_rows)


def _moba_sel_kernel(pt_ref, *refs, H, PP, PPB):
    del pt_ref
    k_refs = refs[0:PP]
    q_ref, sel_ref, g_scr = refs[PP:]
    j = pl.program_id(1)
    T = q_ref.shape[0]
    PAGE = k_refs[0].shape[0]

    @pl.when(j == 0)
    def _():
        g_scr[...] = jnp.zeros_like(g_scr)

    lane = lax.broadcasted_iota(jnp.int32, (H, LANES), 1)
    for bi in range(PP // PPB):
        ks = jnp.sum(k_refs[bi * PPB][...], axis=0)
        for r in range(1, PPB):
            ks = ks + jnp.sum(k_refs[bi * PPB + r][...], axis=0)
        kmean = ks * (1.0 / (PPB * PAGE))
        blk = j * (PP // PPB) + bi
        for t in range(T):
            gcol = jnp.sum(q_ref[t] * kmean, axis=-1, keepdims=True)
            g_scr[t] = jnp.where(lane == blk, gcol, g_scr[t])

    @pl.when(j == pl.num_programs(1) - 1)
    def _():
        n_blk = pl.num_programs(1) * (PP // PPB)
        for t in range(T):
            sel = _top_lanes(g_scr[t], lane < n_blk, MOBA_TOPK)
            sel_ref[t] = sel.astype(F32)


def _moba_sel(mq3, layer, cache_k, page_table):
    DB, T, H, _ = mq3.shape
    PAGE = cache_k.shape[2]
    n_pages = page_table.shape[1]
    PPB = MOBA_BLOCK // PAGE
    PP = PAGES_PER_STEP if n_pages % PAGES_PER_STEP == 0 else PPB
    assert MOBA_BLOCK % PAGE == 0 and PP % PPB == 0 and n_pages % PP == 0 and n_pages // PPB <= LANES
    NS = n_pages // PP

    def k_spec(i):
        return pl.BlockSpec((None, None, PAGE, H, HEAD_DIM),
                            lambda b, j, pt: (layer, pt[b, j * PP + i], 0, 0, 0))

    grid_spec = pltpu.PrefetchScalarGridSpec(
        num_scalar_prefetch=1,
        grid=(DB, NS),
        in_specs=[k_spec(i) for i in range(PP)]
                 + [pl.BlockSpec((None, T, H, HEAD_DIM), lambda b, j, pt: (b, 0, 0, 0))],
        out_specs=pl.BlockSpec((None, T, H, LANES), lambda b, j, pt: (b, 0, 0, 0)),
        scratch_shapes=[pltpu.VMEM((T, H, LANES), F32)],
    )
    return pl.pallas_call(
        functools.partial(_moba_sel_kernel, H=H, PP=PP, PPB=PPB),
        grid_spec=grid_spec,
        out_shape=jax.ShapeDtypeStruct((DB, T, H, LANES), F32),
        compiler_params=_cp(("parallel", "arbitrary")),
        name="moba_sel",
    )(page_table, *([cache_k] * PP), mq3)


def _moba_sample_kernel(pt_ref, *refs, H, PP, PPB):
    del pt_ref
    k_refs = refs[0:PP]
    v_refs = refs[PP:2 * PP]
    qn_ref, kn_ref, vn_ref, sel_ref, o_ref, m_scr, l_scr, acc_scr = refs[2 * PP:]
    j = pl.program_id(1)
    T = qn_ref.shape[0]
    PAGE = k_refs[0].shape[0] // H
    W = PP * PAGE

    @pl.when(j == 0)
    def _():
        m_scr[...] = jnp.full(m_scr.shape, NEG, F32)
        l_scr[...] = jnp.zeros_like(l_scr)
        acc_scr[...] = jnp.zeros_like(acc_scr)

    expand = (lax.broadcasted_iota(jnp.int32, (LANES, W), 0)
              == j * (PP // PPB) + lax.broadcasted_iota(jnp.int32, (LANES, W), 1) // (PPB * PAGE)).astype(BF16)
    oks = [jnp.dot(sel_ref[h].astype(BF16), expand, preferred_element_type=F32) > 0.5 for h in range(H)]

    def adjust(h, i, sb):
        ok = oks[h][:, i * LANES:(i + 1) * LANES]
        return jnp.where(ok, sb, NEG), ok

    _paged_step(H, qn_ref, k_refs, v_refs, adjust, m_scr, l_scr, acc_scr)

    @pl.when(j == pl.num_programs(1) - 1)
    def _():
        zero = jnp.zeros((1, 1), F32)
        for h in range(H):
            q = qn_ref[:, h * HEAD_DIM:(h + 1) * HEAD_DIM] * (HEAD_DIM ** -0.5)
            l, acc = _new_token_columns(h, q, kn_ref, vn_ref, zero, lambda s_: zero,
                                        m_scr[h], l_scr[h], acc_scr[h])
            o_ref[:, h * HEAD_DIM:(h + 1) * HEAD_DIM] = (acc / l).astype(BF16)


def _moba_sample(mq, mk, proj, lay, DB, T, layer, cache_k, cache_v, page_table, sel):
    GB = lay["GB"]
    H = GB
    G = H * HEAD_DIM
    PAGE = cache_k.shape[2]
    n_pages = page_table.shape[1]
    PPB = MOBA_BLOCK // PAGE
    PP = PAGES_PER_STEP if n_pages % PAGES_PER_STEP == 0 else PPB
    NS = n_pages // PP
    assert (n_pages * PAGE) % MOBA_BLOCK == 0 and T <= MOBA_BLOCK

    k_rows, page_spec = _paged_rows(cache_k)
    v_rows, _ = _paged_rows(cache_v)

    def kv_spec(i):
        return page_spec(layer, lambda b, j, pt: pt[b, j * PP + i])

    grid_spec = pltpu.PrefetchScalarGridSpec(
        num_scalar_prefetch=1,
        grid=(DB, NS),
        in_specs=([kv_spec(i) for i in range(PP)] + [kv_spec(i) for i in range(PP)]
                  + [pl.BlockSpec((T, G), lambda b, j, pt: (b, 0)),
                     pl.BlockSpec((T, G), lambda b, j, pt: (b, 0)),
                     pl.BlockSpec((T, G), lambda b, j, pt: (b, 9)),
                     pl.BlockSpec((None, H, T, LANES), lambda b, j, pt: (b, 0, 0, 0))]),
        out_specs=pl.BlockSpec((T, G), lambda b, j, pt: (b, 0)),
        scratch_shapes=[pltpu.VMEM((H, T, LANES), F32), pltpu.VMEM((H, T, LANES), F32),
                        pltpu.VMEM((H, T, HEAD_DIM), F32)],
    )
    return pl.pallas_call(
        functools.partial(_moba_sample_kernel, H=H, PP=PP, PPB=PPB),
        grid_spec=grid_spec,
        out_shape=jax.ShapeDtypeStruct((DB * T, G), BF16),
        compiler_params=_cp(("parallel", "arbitrary")),
        name="moba_sample",
    )(page_table, *([k_rows] * PP), *([v_rows] * PP), mq, mk, proj, sel)


RW_R, RW_W, RW_K, RW_KK, RW_KKA, RW_WR, RW_N = 0, 1, 2, 3, 4, 5, 6


def _seg_sum(x):
    bd = (lax.broadcasted_iota(jnp.int32, (LANES, LANES), 0) // RWKV_HEAD
          == lax.broadcasted_iota(jnp.int32, (LANES, LANES), 1) // RWKV_HEAD).astype(F32)
    outs = [jnp.dot(x[:, c * LANES:(c + 1) * LANES], bd, precision=HIGHEST, preferred_element_type=F32)
            for c in range(x.shape[1] // LANES)]
    return outs[0] if len(outs) == 1 else jnp.concatenate(outs, axis=1)


def _rwkv_prep_kernel(r_ref, k_ref, v_ref, d_ref, rp_ref, kp_ref, vp_ref, dp_ref,
                      s0m_ref, s0d_ref, mum_ref, mud_ref,
                      wup_ref, aup_ref, gup_ref, w0_ref, a0_ref, kk_ref, ka_ref, rk_ref,
                      kvo, vo, go, bo):
    t = pl.program_id(1)
    tt, G = r_ref.shape
    first = t == 0

    def mixed(cur_ref, prev_ref, s0, mu):
        cur = cur_ref[...]
        n = cur.shape[0]
        prow = jnp.where(first, s0, prev_ref[prev_ref.shape[0] - 1:, :])
        row = lax.broadcasted_iota(jnp.int32, cur.shape, 0)
        prev = jnp.where(row == 0, prow, pltpu.roll(cur, 1, axis=0)) if n > 1 else prow
        return cur + (prev - cur) * mu

    r = mixed(r_ref, rp_ref, s0m_ref[:, 0:G], mum_ref[:, 0:G])
    kr = mixed(k_ref, kp_ref, s0m_ref[:, G:2 * G], mum_ref[:, G:2 * G])
    vr = mixed(v_ref, vp_ref, s0m_ref[:, 2 * G:3 * G], mum_ref[:, 2 * G:3 * G])
    d = mixed(d_ref, dp_ref, s0d_ref[...], mud_ref[...])

    dwa = d[:, 0:LANES]
    tw = jnp.dot(jnp.tanh(dwa).astype(BF16), wup_ref[...], preferred_element_type=F32)
    w_raw = -_softplus(-(w0_ref[...] + tw)) - 0.5
    w = jnp.exp(-jnp.exp(w_raw))
    a = jax.nn.sigmoid(a0_ref[...] + jnp.dot(dwa.astype(BF16), aup_ref[...], preferred_element_type=F32))
    g = jnp.dot(jax.nn.sigmoid(d[:, LANES:]).astype(BF16), gup_ref[...], preferred_element_type=F32)
    kk = kr * kk_ref[...]
    kk = kk / jnp.maximum(jnp.sqrt(_seg_sum(kk * kk)), 1e-12)
    k2 = kr * (1.0 + (a - 1.0) * ka_ref[...])
    kvo[RW_R] = r
    kvo[RW_W] = w
    kvo[RW_K] = k2
    kvo[RW_KK] = kk
    kvo[RW_KKA] = kk * a
    kvo[RW_WR] = w * r
    vo[...] = vr
    go[...] = g
    bo[...] = _seg_sum(r * k2 * rk_ref[...]) * vr


def _rwkv_prep(proj, lay, B, T, shift_main, shift_down, mu_main, mu_down, wts):
    GB = lay["GB"]
    G = GB * LANES
    tt = _pick(T, 256, 8)
    nt = T // tt
    dcol = lay["DOWN"]
    assert (dcol * LANES) % (3 * LANES) == 0
    dblk = dcol // 3

    def cur(idx, w):
        return pl.BlockSpec((tt, w), lambda b, t: (b * nt + t, idx))

    def prev(idx, w):
        return pl.BlockSpec((8, w), lambda b, t: (jnp.maximum((b * T + t * tt) // 8 - 1, 0), idx))

    full = lambda a: pl.BlockSpec(a.shape, lambda b, t: (0,) * a.ndim)
    ospec = pl.BlockSpec((tt, G), lambda b, t: (b * nt + t, 0))
    outs = pl.pallas_call(
        _rwkv_prep_kernel,
        grid=(B, nt),
        in_specs=[cur(10, G), cur(11, G), cur(12, G), cur(dblk, 3 * LANES),
                  prev(10, G), prev(11, G), prev(12, G), prev(dblk, 3 * LANES),
                  pl.BlockSpec((None, 1, 3 * G), lambda b, t: (b, 0, 0)),
                  pl.BlockSpec((None, 1, 3 * LANES), lambda b, t: (b, 0, 0)),
                  full(mu_main), full(mu_down)] + [full(a) for a in wts],
        out_specs=[pl.BlockSpec((RW_N, tt, G), lambda b, t: (0, b * nt + t, 0))] + [ospec] * 3,
        out_shape=[jax.ShapeDtypeStruct((RW_N, B * T, G), F32)] + [jax.ShapeDtypeStruct((B * T, G), F32)] * 3,
        compiler_params=_cp(("parallel", "arbitrary")),
        name="rwkv_prep",
    )(proj, proj, proj, proj, proj, proj, proj, proj, shift_main, shift_down, mu_main, mu_down, *wts)
    return outs


def _rwkv_scan_kernel(kv_ref, v_ref, s0_ref, y_ref, so_ref, s_scr):
    i = pl.program_id(0)
    _, tt, K, L = kv_ref.shape

    @pl.when(i == 0)
    def _():
        s_scr[...] = s0_ref[...]

    def row(a, t, kidx):
        return kv_ref[a, t, pl.ds(kidx, 1), :]

    vshape = v_ref.shape[1:]
    sa0 = jnp.zeros(vshape, F32)
    ya0 = jnp.zeros(vshape, F32)
    for kidx in range(K):
        s = s_scr[kidx]
        sa0 = sa0 - s * row(RW_KK, 0, kidx)
        ya0 = ya0 + s * row(RW_WR, 0, kidx)

    def token(t, carry):
        sa, ya = carry
        vv = v_ref[t]
        rt = kv_ref[RW_R, t]
        kr_dot = jnp.sum(kv_ref[RW_K, t] * rt, axis=0, keepdims=True)
        kar_dot = jnp.sum(kv_ref[RW_KKA, t] * rt, axis=0, keepdims=True)
        y_ref[t] = ya + sa * kar_dot + vv * kr_dot
        tn = jnp.minimum(t + 1, tt - 1)
        sa_n = jnp.zeros(vshape, F32)
        ya_n = jnp.zeros(vshape, F32)
        for kidx in range(K):
            s = s_scr[kidx] * row(RW_W, t, kidx) + sa * row(RW_KKA, t, kidx) + vv * row(RW_K, t, kidx)
            s_scr[kidx] = s
            sa_n = sa_n - s * row(RW_KK, tn, kidx)
            ya_n = ya_n + s * row(RW_WR, tn, kidx)
        return sa_n, ya_n

    lax.fori_loop(0, tt, token, (sa0, ya0))

    @pl.when(i == pl.num_programs(0) - 1)
    def _():
        so_ref[...] = s_scr[...]


def _rwkv_scan(kv, v, s0):
    _, T, K, L = kv.shape
    VP = v.shape[1]
    tt = _pick(T, 32, 1)
    vspec = pl.BlockSpec((tt, VP, L), lambda i: (i, 0, 0))
    sspec = pl.BlockSpec((K, VP, L), lambda i: (0, 0, 0))
    return pl.pallas_call(
        _rwkv_scan_kernel,
        grid=(T // tt,),
        in_specs=[pl.BlockSpec((RW_N, tt, K, L), lambda i: (0, i, 0, 0)), vspec, sspec],
        out_specs=[vspec, sspec],
        out_shape=[jax.ShapeDtypeStruct((T, VP, L), F32), jax.ShapeDtypeStruct((K, VP, L), F32)],
        scratch_shapes=[pltpu.VMEM((K, VP, L), F32)],
        compiler_params=_cp(("arbitrary",)),
        name="rwkv_scan",
    )(kv, v, s0)


def _rwkv_post_kernel(y_ref, b_ref, g_ref, lw_ref, lb_ref, o_ref):
    y = y_ref[...]
    mu = _seg_sum(y) * (1.0 / RWKV_HEAD)
    yc = y - mu
    var = _seg_sum(yc * yc) * (1.0 / RWKV_HEAD)
    yn = yc * lax.rsqrt(var + RWKV_LN_EPS) * lw_ref[...] + lb_ref[...]
    o_ref[...] = ((yn + b_ref[...]) * g_ref[...]).astype(BF16)


def _rwkv_post(y, bonus, g, ln_w, ln_b):
    M, G = y.shape
    tm = _pick(M, 512, 8)
    spec = pl.BlockSpec((tm, G), lambda i: (i, 0))
    rspec = pl.BlockSpec((1, G), lambda i: (0, 0))
    return pl.pallas_call(
        _rwkv_post_kernel,
        grid=(M // tm,),
        in_specs=[spec, spec, spec, rspec, rspec],
        out_specs=spec,
        out_shape=jax.ShapeDtypeStruct((M, G), BF16),
        compiler_params=_cp(("parallel",)),
        name="rwkv_post",
    )(y, bonus, g, ln_w.reshape(1, G), ln_b.reshape(1, G))


def _rwkv(proj, lay, B, T, shift0, state0, p, l):
    GB = lay["GB"]
    G = GB * LANES
    RH = G // RWKV_HEAD
    N = RWKV_HEAD
    chains = B * RH
    assert LANES % chains == 0 or chains % LANES == 0
    VH = max(1, LANES // chains)
    VP = N // VH
    rw = p["rwkv_w_up"][l].shape[0]
    ra = p["rwkv_a_up"][l].shape[0]
    rg = p["rwkv_g_up"][l].shape[0]
    assert rw + ra == LANES and rg <= 2 * LANES

    def padrows(a, top, total):
        return jnp.pad(a, ((top, total - top - a.shape[0]), (0, 0))).astype(BF16)

    def pad_down(a):
        return jnp.pad(a, [(0, 0)] * (a.ndim - 1) + [(0, 3 * LANES - a.shape[-1])])

    wts = (padrows(p["rwkv_w_up"][l], 0, LANES), padrows(p["rwkv_a_up"][l], rw, LANES),
           padrows(p["rwkv_g_up"][l], 0, 2 * LANES),
           p["rwkv_w0"][l].reshape(1, G), p["rwkv_a0"][l].reshape(1, G),
           p["rwkv_k_k"][l].reshape(1, G), p["rwkv_k_a"][l].reshape(1, G), p["rwkv_r_k"][l].reshape(1, G))
    mu = p["rwkv_mu"][l]
    kv, v, g, bonus = _rwkv_prep(
        proj, lay, B, T,
        shift0[:, None, :3 * G], pad_down(shift0[:, None, 3 * G:]),
        mu[None, :3 * G], pad_down(mu[None, 3 * G:]), wts)

    kv_t = jnp.broadcast_to(kv.reshape(RW_N, 1, B, T, RH, N), (RW_N, VH, B, T, RH, N))
    kv_t = kv_t.transpose(0, 3, 5, 1, 2, 4).reshape(RW_N, T, N, VH * chains)
    v_t = v.reshape(B, T, RH, VH, VP).transpose(1, 4, 3, 0, 2).reshape(T, VP, VH * chains)
    s0_t = state0.reshape(B, RH, VH, VP, N).transpose(4, 3, 2, 0, 1).reshape(N, VP, VH * chains)
    y_t, s_t = _rwkv_scan(kv_t, v_t, s0_t)
    y = y_t.reshape(T, VP, VH, B, RH).transpose(3, 0, 4, 2, 1).reshape(B * T, G)
    s_new = s_t.reshape(N, VP, VH, B, RH).transpose(3, 4, 2, 1, 0).reshape(B, RH, N, N)
    mix = _rwkv_post(y, bonus, g, p["rwkv_ln_w"][l], p["rwkv_ln_b"][l])
    return mix, s_new


def _peer_topk_kernel(q_ref, keys_ref, e_ref, g_ref, v1_scr, i1_scr, v2_scr, i2_scr, c_scr, ci_scr, t_scr, te_scr):
    NK = keys_ref.shape[2]
    KH = keys_ref.shape[3]
    tn = q_ref.shape[0]
    TK = PEER_TOPK
    big = jnp.int32(1 << 30)

    def top_rows(s, iota, vals_scr, idx_scr, payload=None):
        for kk in range(TK):
            m = jnp.max(s, axis=0, keepdims=True)
            first = jnp.min(jnp.where(s == m, iota, big), axis=0, keepdims=True)
            hit = iota == first
            vals_scr[kk:kk + 1, :] = m
            if payload is None:
                idx_scr[kk:kk + 1, :] = first
            else:
                idx_scr[kk:kk + 1, :] = jnp.max(jnp.where(hit, payload, -1), axis=0, keepdims=True)
            s = jnp.where(hit, -jnp.inf, s)

    iota_k = lax.broadcasted_iota(jnp.int32, (NK, tn), 0)
    iota_c = lax.broadcasted_iota(jnp.int32, (TK * TK, tn), 0)
    for h in range(PEER_HEADS):
        for half, (vs, is_) in enumerate(((v1_scr, i1_scr), (v2_scr, i2_scr))):
            c0 = (h * 2 + half) * KH
            s = _dot_nt(keys_ref[h, half], q_ref[:, c0:c0 + KH], precision=HIGHEST)
            top_rows(s, iota_k, vs, is_)
        v2 = v2_scr[...]
        i2 = i2_scr[...]
        for a in range(TK):
            c_scr[a * TK:(a + 1) * TK, :] = v1_scr[a:a + 1, :] + v2
            ci_scr[a * TK:(a + 1) * TK, :] = i1_scr[a:a + 1, :] * NK + i2
        top_rows(c_scr[...], iota_c, t_scr, te_scr, payload=ci_scr[...])
        top = t_scr[...]
        ex = jnp.exp(top - top[0:1, :])
        g_ref[h * TK:(h + 1) * TK, :] = ex / jnp.sum(ex, axis=0, keepdims=True)
        e_ref[h * TK:(h + 1) * TK, :] = te_scr[...]


def _peer_topk(qp, keys):
    M = qp.shape[0]
    NK = keys.shape[2]
    tn = _pick(M, 256)
    J = PEER_HEADS * PEER_TOPK
    TK = PEER_TOPK
    return pl.pallas_call(
        _peer_topk_kernel,
        grid=(M // tn,),
        in_specs=[pl.BlockSpec((tn, qp.shape[1]), lambda i: (i, 0)),
                  pl.BlockSpec(keys.shape, lambda i: (0, 0, 0, 0))],
        out_specs=[pl.BlockSpec((J, tn), lambda i: (0, i))] * 2,
        out_shape=[jax.ShapeDtypeStruct((J, M), jnp.int32), jax.ShapeDtypeStruct((J, M), F32)],
        scratch_shapes=[pltpu.VMEM((TK, tn), F32), pltpu.VMEM((TK, tn), jnp.int32),
                        pltpu.VMEM((TK, tn), F32), pltpu.VMEM((TK, tn), jnp.int32),
                        pltpu.VMEM((TK * TK, tn), F32), pltpu.VMEM((TK * TK, tn), jnp.int32),
                        pltpu.VMEM((TK, tn), F32), pltpu.VMEM((TK, tn), jnp.int32)],
        compiler_params=_cp(("parallel",)),
        name="peer_topk",
    )(qp, keys)


def _peer_w_kernel(e0_ref, g0_ref, e1_ref, g1_ref, w_ref, *, NK):
    tp, J = e0_ref.shape
    iota = lax.broadcasted_iota(jnp.int32, (NK, J), 0)
    shift = NK.bit_length() - 1
    assert 1 << shift == NK
    sub = lax.broadcasted_iota(jnp.int32, (NK // 8, 8, NK), 1)

    def table(e_ref, g_ref, n):
        e = e_ref[pl.ds(n, 1), :]
        g = g_ref[pl.ds(n, 1), :]
        a = e >> shift
        b = e & (NK - 1)
        at = jnp.where(iota == a, g, 0.0).astype(BF16)
        bt = (iota == b).astype(BF16)
        return _dot_nt(at, bt).reshape(NK // 8, 8, NK)

    def pair(m, carry):
        w0 = table(e0_ref, g0_ref, m)
        w1 = table(e1_ref, g1_ref, m)
        lo = jnp.where(sub < 4, w0, pltpu.roll(w1, 4, axis=1))
        hi = jnp.where(sub < 4, pltpu.roll(w0, 4, axis=1), w1)
        w_ref[:, pl.ds(pl.multiple_of(m * 8, 8), 8), :] = jnp.stack([lo, hi], axis=1).reshape(NK // 4, 8, NK)
        return carry

    lax.fori_loop(0, tp, pair, 0, unroll=2)


def _peer_w(eidx, gates, NK, tw):
    M, J = eidx.shape
    half = tw // 2
    tp = _pick(half, 32, 8)
    assert NK % 8 == 0 and tw % 2 == 0 and M % tw == 0
    nc = half // tp
    first = pl.BlockSpec((tp, J), lambda i, c: (i * 2 * nc + c, 0))
    second = pl.BlockSpec((tp, J), lambda i, c: (i * 2 * nc + nc + c, 0))
    return pl.pallas_call(
        functools.partial(_peer_w_kernel, NK=NK),
        grid=(M // tw, nc),
        in_specs=[first, first, second, second],
        out_specs=pl.BlockSpec((NK // 4, tp * 8, NK), lambda i, c: (0, i * nc + c, 0)),
        out_shape=jax.ShapeDtypeStruct((NK // 4, M // 2 * 8, NK), F32),
        compiler_params=_cp(("parallel", "parallel")),
        name="peer_w",
    )(eidx, gates, eidx, gates)


def _peer_dense_kernel(h_ref, u_ref, w_ref, v_ref, x_ref, ga_ref, o_ref):
    e = pl.program_id(2)

    @pl.when(e == 0)
    def _():
        o_ref[...] = jnp.zeros_like(o_ref)

    a = _dot_nt(h_ref[...], u_ref[...])
    act = 0.5 * a * (1.0 + lax.erf(a * (2.0 ** -0.5)))
    NK = w_ref.shape[1]
    half = w_ref.shape[0] // 8
    zs = []
    for r in range(4):
        wr = jnp.concatenate([w_ref[pl.ds(r, half, stride=8), :], w_ref[pl.ds(4 + r, half, stride=8), :]], axis=0)
        zs.append((act[:, r * NK:(r + 1) * NK] * wr).astype(BF16))
    z = jnp.concatenate(zs, axis=1)
    o_ref[...] += jnp.dot(z, v_ref[...], preferred_element_type=F32).reshape(o_ref.shape)

    @pl.when(e == pl.num_programs(2) - 1)
    def _():
        o_ref[...] = x_ref[...] + ga_ref[...] * o_ref[...]


def _peer_dense(x3, h2, w4, u, v, mod3, ga_idx):
    B, T, D = x3.shape
    E = u.shape[0]
    bb, tt = _row_tiles(B, T)
    NK = w4.shape[2]
    te = 4 * NK
    nt = T // tt
    half = bb * tt // 2
    assert w4.shape == (NK // 4, B * T * 4, NK) and E == NK * NK
    once = pl.Buffered(1)
    return pl.pallas_call(
        _peer_dense_kernel,
        grid=(B // bb, nt, E // te),
        in_specs=[pl.BlockSpec((bb * tt, D), lambda b, t, e: (b * nt + t, 0), pipeline_mode=once),
                  pl.BlockSpec((te, D), lambda b, t, e: (e, 0)),
                  pl.BlockSpec((None, half * 8, NK), lambda b, t, e: (e, b * nt + t, 0)),
                  pl.BlockSpec((te, D), lambda b, t, e: (e, 0)),
                  pl.BlockSpec((bb, tt, D), lambda b, t, e: (b, t, 0), pipeline_mode=once),
                  pl.BlockSpec((bb, 1, D), lambda b, t, e: (b, 0, ga_idx))],
        out_specs=pl.BlockSpec((bb, tt, D), lambda b, t, e: (b, t, 0)),
        out_shape=jax.ShapeDtypeStruct((B, T, D), F32),
        compiler_params=_cp(("parallel", "parallel", "arbitrary")),
        name="peer_dense",
    )(h2, u, w4, v, x3, mod3)


def _layout(G):
    GB = G // LANES
    return {"GB": GB, "SMALL_A": 13 * GB, "DOWN": 13 * GB + 1, "NP": (13 * GB + 4) * LANES}


def _pack_w_in(w_in, G, gate_rank, n_fox_heads, n_down):
    gla, fox = 0, 4 * G + gate_rank
    moba = fox + 3 * G + n_fox_heads
    rw = moba + 3 * G
    D = w_in.shape[0]
    zeros = lambda n: jnp.zeros((D, n), w_in.dtype)
    parts = [w_in[:, gla:gla + 4 * G], w_in[:, fox:fox + 3 * G], w_in[:, moba:moba + 3 * G], w_in[:, rw:rw + 3 * G],
             w_in[:, 4 * G:4 * G + gate_rank], w_in[:, fox + 3 * G:fox + 3 * G + n_fox_heads],
             zeros(LANES - gate_rank - n_fox_heads),
             w_in[:, rw + 3 * G:rw + 3 * G + n_down], zeros(3 * LANES - n_down)]
    return jnp.concatenate(parts, axis=1).astype(BF16)


def _layer(x3, mod3, l, p, wl, past, page_table):
    B, T, D = x3.shape
    G = D // 4
    lay = _layout(G)
    GB = lay["GB"]
    H = GB
    gate_rank = p["gla_gate_up"].shape[1]
    M = B * T

    proj = _norm_mod_matmul(x3, mod3, 1, 0, p["norm_mix"][l], wl["w_in"], emit_h=False, name="in_proj")

    mix_gla, gla_s = _gla(proj, lay, B, T, wl["gup"], p["gla_gate_b"][l], p["gla_norm"][l], past["gla"])

    lf, cum = _fox_cum(proj, lay, B, T, wl["fb_row"])
    lf8 = lf[:, gate_rank:gate_rank + H].reshape(B, T, H)
    cum8 = cum[:, gate_rank:gate_rank + H].reshape(B, T, H)
    if page_table is None:
        mix_fox = _fox_prompt(proj, lay, B, T, cum8.transpose(0, 2, 1)[..., None], cum8.transpose(0, 2, 1)[:, :, None, :])
    else:
        cum_rows = cum8.transpose(0, 2, 1)
        mix_fox = _fox_sample(proj, lay, B, T, l, past["fox_k"], past["fox_v"], past["fox_lf_t"], page_table,
                              jnp.broadcast_to(cum_rows[..., None], (B, H, T, LANES)), cum_rows)

    pos0 = 0 if page_table is None else page_table.shape[1] * past["moba_k"].shape[2]
    mq, mk = _rope(proj, lay, B, T, pos0)
    if page_table is None:
        mix_moba = _moba_prompt(mq, mk, proj, lay, B, T)
    else:
        sel = _moba_sel(mq.reshape(B, T, H, HEAD_DIM), l, past["moba_k"], page_table)
        mix_moba = _moba_sample(mq, mk, proj, lay, B, T, l, past["moba_k"], past["moba_v"], page_table,
                                sel.transpose(0, 2, 1, 3))

    mix_rwkv, rwkv_s = _rwkv(proj, lay, B, T, past["shift"], past["rwkv"], p, l)

    x3 = _out_proj(x3, (mix_gla, mix_fox, mix_moba, mix_rwkv), wl["w_out"], mod3, 2)

    qp, h2 = _norm_mod_matmul(x3, mod3, 4, 3, p["norm_ffn"][l], wl["wq"], emit_h=True, name="peer_q")
    NK = p["peer_keys"].shape[3]
    eidx_t, gate_t = _peer_topk(qp, p["peer_keys"][l])
    bb, tt = _row_tiles(B, T)
    w4 = _peer_w(eidx_t.T, gate_t.T, NK, bb * tt)
    x3 = _peer_dense(x3, h2, w4, wl["u"], wl["v"], mod3, 5)

    def heads(c0):
        return proj[:, c0 * LANES:(c0 + GB) * LANES].reshape(B, T, H, HEAD_DIM)

    last = proj.reshape(B, T, -1)[:, T - 1]
    n_down = p["rwkv_mu"].shape[1] - 3 * G
    shift = jnp.concatenate([last[:, 10 * G:13 * G], last[:, lay["DOWN"] * LANES:lay["DOWN"] * LANES + n_down]], axis=-1)
    new = (heads(5 * GB), heads(6 * GB), lf8, mk.reshape(B, T, H, HEAD_DIM), heads(9 * GB), gla_s, rwkv_s, shift)
    return x3, new


def kernel(x_prompt, x_sample, cache_fox_k, cache_fox_v, cache_fox_logf, cache_moba_k, cache_moba_v, state_gla, state_rwkv, state_rwkv_shift, page_table, c_prompt, c_sample, w_mod, b_mod, norm_mix, norm_ffn, w_in, gla_gate_up, gla_gate_b, gla_norm, fox_forget_b, rwkv_mu, rwkv_w0, rwkv_w_up, rwkv_a0, rwkv_a_up, rwkv_g_up, rwkv_k_k, rwkv_k_a, rwkv_r_k, rwkv_ln_w, rwkv_ln_b, w_out, peer_wq, peer_keys, peer_u, peer_v, final_norm):
    p = dict(norm_mix=norm_mix, norm_ffn=norm_ffn, gla_gate_up=gla_gate_up, gla_gate_b=gla_gate_b,
             gla_norm=gla_norm, rwkv_mu=rwkv_mu, rwkv_w0=rwkv_w0, rwkv_w_up=rwkv_w_up, rwkv_a0=rwkv_a0,
             rwkv_a_up=rwkv_a_up, rwkv_g_up=rwkv_g_up, rwkv_k_k=rwkv_k_k, rwkv_k_a=rwkv_k_a, rwkv_r_k=rwkv_r_k,
             rwkv_ln_w=rwkv_ln_w, rwkv_ln_b=rwkv_ln_b, peer_keys=peer_keys)
    L, D, _ = w_mod.shape
    G = D // 4
    B, T, _ = x_prompt.shape
    DB, DT, _ = x_sample.shape
    H = G // HEAD_DIM
    RH = G // RWKV_HEAD
    gate_rank = gla_gate_up.shape[1]
    n_down = rwkv_mu.shape[1] - 3 * G

    R = -(-(DB + B) // 8) * 8
    c_all = jnp.concatenate([c_sample, c_prompt, jnp.zeros((R - DB - B, D), F32)], axis=0)
    mod = _adaln_mod(c_all, w_mod, b_mod)

    fox_lf_t = cache_fox_logf.transpose(0, 1, 3, 2)
    xp, xs = x_prompt, x_sample
    news_p, news_s = [], []
    for l in range(L):
        wl = {
            "w_in": _pack_w_in(w_in[l], G, gate_rank, H, n_down),
            "w_out": w_out[l].astype(BF16),
            "wq": peer_wq[l].astype(BF16),
            "u": peer_u[l].astype(BF16),
            "v": peer_v[l].astype(BF16),
            "gup": jnp.pad(gla_gate_up[l], ((0, LANES - gate_rank), (0, 0))).astype(BF16),
            "fb_row": jnp.pad(fox_forget_b[l], (gate_rank, LANES - gate_rank - H)).reshape(1, LANES),
        }
        past_p = {"gla": jnp.zeros((B, H, HEAD_DIM, HEAD_DIM), F32),
                  "rwkv": jnp.zeros((B, RH, RWKV_HEAD, RWKV_HEAD), F32),
                  "shift": jnp.zeros((B, 3 * G + n_down), F32)}
        past_s = {"gla": state_gla[l], "rwkv": state_rwkv[l], "shift": state_rwkv_shift[l],
                  "fox_k": cache_fox_k, "fox_v": cache_fox_v, "fox_lf_t": fox_lf_t,
                  "moba_k": cache_moba_k, "moba_v": cache_moba_v}
        xp, new_p = _layer(xp, mod[l, DB:DB + B, None, :], l, p, wl, past_p, None)
        xs, new_s = _layer(xs, mod[l, 0:DB, None, :], l, p, wl, past_s, page_table)
        news_p.append(new_p)
        news_s.append(new_s)
    y_prompt = _final_norm(xp, final_norm)
    y_sample = _final_norm(xs, final_norm)
    stack = lambda news: [jnp.stack([n[i] for n in news]) for i in range(8)]
    return (y_prompt, y_sample, *stack(news_p), *stack(news_s))
```

```python
import functools

import jax
import jax.numpy as jnp
from jax import lax
from jax.experimental import pallas as pl
from jax.experimental.pallas import tpu as pltpu

F32 = jnp.float32
BF16 = jnp.bfloat16
HIGHEST = lax.Precision.HIGHEST

LANES = 128
HEAD_DIM = 128
RWKV_HEAD = 64
N_MOD = 6
RMS_EPS = 1e-6
GLA_TAU = 16.0
GLA_CHUNK = 64
GLA_SUB = 16
MOBA_BLOCK = 256
MOBA_TOPK = 3
ROPE_THETA = 10000.0
RWKV_LN_EPS = 64e-5
PEER_HEADS = 8
PEER_TOPK = 16
NEG = -1e30
VMEM_LIMIT = 52 * 1024 * 1024
PAGES_PER_STEP = 8


def _cp(sem, vmem=VMEM_LIMIT):
    return pltpu.CompilerParams(dimension_semantics=sem, vmem_limit_bytes=vmem)


def _pick(n, pref, mult=LANES):
    best = None
    d = mult
    while d <= min(n, pref):
        if n % d == 0:
            best = d
        d += mult
    return best if best is not None else n


def _row_tiles(B, T):
    if T >= 256:
        return 1, _pick(T, 512, 8)
    return B, T


def _log_sigmoid(z):
    return jnp.minimum(z, 0.0) - jnp.log(1.0 + jnp.exp(-jnp.abs(z)))


def _softplus(z):
    return jnp.maximum(z, 0.0) + jnp.log(1.0 + jnp.exp(-jnp.abs(z)))


def _cumsum_rows(x):
    n = x.shape[0]
    row = lax.broadcasted_iota(jnp.int32, x.shape, 0)
    s = 1
    while s < n:
        x = x + jnp.where(row >= s, pltpu.roll(x, s, axis=0), 0.0)
        s *= 2
    return x


def _dot_nt(a, b, **kw):
    return lax.dot_general(a, b, (((1,), (1,)), ((), ())), preferred_element_type=F32, **kw)


def _dot_tn(a, b):
    return lax.dot_general(a, b, (((0,), (0,)), ((), ())), preferred_element_type=F32)


def _mod_kernel(c_ref, w_ref, b_ref, o_ref):
    c = c_ref[...]
    ca = (c * jax.nn.sigmoid(c)).astype(BF16)
    o_ref[...] = jnp.dot(ca, w_ref[...].astype(BF16), preferred_element_type=F32) + b_ref[...]


def _adaln_mod(c_all, w_mod, b_mod):
    L, D, N = w_mod.shape
    R = c_all.shape[0]
    tn = _pick(N, 512)
    return pl.pallas_call(
        _mod_kernel,
        grid=(L, N // tn),
        in_specs=[pl.BlockSpec((R, D), lambda l, n: (0, 0)),
                  pl.BlockSpec((None, D, tn), lambda l, n: (l, 0, n)),
                  pl.BlockSpec((None, 1, tn), lambda l, n: (l, 0, n))],
        out_specs=pl.BlockSpec((None, R, tn), lambda l, n: (l, 0, n)),
        out_shape=jax.ShapeDtypeStruct((L, R, N), F32),
        compiler_params=_cp(("parallel", "parallel")),
        name="adaln_mod",
    )(c_all, w_mod, b_mod.reshape(L, 1, N))


def _nm_kernel(x_ref, sc_ref, sh_ref, gain_ref, w_ref, o_ref, *rest, emit_h):
    if emit_h:
        h_out_ref, h_scr = rest
    else:
        (h_scr,) = rest

    @pl.when(pl.program_id(2) == 0)
    def _():
        x = x_ref[...]
        ms = jnp.mean(x * x, axis=-1, keepdims=True)
        y = x * lax.rsqrt(ms + RMS_EPS) * gain_ref[...]
        h = y * (1.0 + sc_ref[...]) + sh_ref[...]
        hb = h.reshape(h_scr.shape).astype(BF16)
        h_scr[...] = hb
        if emit_h:
            h_out_ref[...] = hb

    o_ref[...] = jnp.dot(h_scr[...], w_ref[...], preferred_element_type=F32)


def _norm_mod_matmul(x3, mod3, sc_idx, sh_idx, gain, w, *, emit_h, name):
    B, T, D = x3.shape
    N = w.shape[1]
    bb, tt = _row_tiles(B, T)
    tn = _pick(N, 512)
    M = B * T
    nt = T // tt
    out_shape = [jax.ShapeDtypeStruct((M, N), F32)]
    out_specs = [pl.BlockSpec((bb * tt, tn), lambda b, t, n: (b * nt + t, n))]
    if emit_h:
        out_shape.append(jax.ShapeDtypeStruct((M, D), BF16))
        out_specs.append(pl.BlockSpec((bb * tt, D), lambda b, t, n: (b * nt + t, 0)))
    res = pl.pallas_call(
        functools.partial(_nm_kernel, emit_h=emit_h),
        grid=(B // bb, nt, N // tn),
        in_specs=[pl.BlockSpec((bb, tt, D), lambda b, t, n: (b, t, 0)),
                  pl.BlockSpec((bb, 1, D), lambda b, t, n: (b, 0, sc_idx)),
                  pl.BlockSpec((bb, 1, D), lambda b, t, n: (b, 0, sh_idx)),
                  pl.BlockSpec((1, D), lambda b, t, n: (0, 0)),
                  pl.BlockSpec((D, tn), lambda b, t, n: (0, n))],
        out_specs=out_specs,
        out_shape=out_shape,
        scratch_shapes=[pltpu.VMEM((bb * tt, D), BF16)],
        compiler_params=_cp(("parallel", "parallel", "arbitrary")),
        name=name,
    )(x3, mod3, mod3, gain.reshape(1, D), w)
    return res if emit_h else res[0]


def _op_kernel(x_ref, m0, m1, m2, m3, w_ref, ga_ref, o_ref):
    G = m0.shape[1]
    acc = jnp.dot(m0[...], w_ref[0:G, :], preferred_element_type=F32)
    acc += jnp.dot(m1[...], w_ref[G:2 * G, :], preferred_element_type=F32)
    acc += jnp.dot(m2[...], w_ref[2 * G:3 * G, :], preferred_element_type=F32)
    acc += jnp.dot(m3[...], w_ref[3 * G:4 * G, :], preferred_element_type=F32)
    o_ref[...] = x_ref[...] + ga_ref[...] * acc.reshape(x_ref.shape)


def _out_proj(x3, mixes, w_out, mod3, ga_idx):
    B, T, D = x3.shape
    G = mixes[0].shape[1]
    bb, tt = _row_tiles(B, T)
    tn = _pick(D, 1024)
    nt = T // tt
    nd = D // tn
    mspec = pl.BlockSpec((bb * tt, G), lambda b, t, n: (b * nt + t, 0))
    return pl.pallas_call(
        _op_kernel,
        grid=(B // bb, nt, nd),
        in_specs=[pl.BlockSpec((bb, tt, tn), lambda b, t, n: (b, t, n)),
                  mspec, mspec, mspec, mspec,
                  pl.BlockSpec((D, tn), lambda b, t, n: (0, n)),
                  pl.BlockSpec((bb, 1, tn), lambda b, t, n: (b, 0, ga_idx * nd + n))],
        out_specs=pl.BlockSpec((bb, tt, tn), lambda b, t, n: (b, t, n)),
        out_shape=jax.ShapeDtypeStruct((B, T, D), F32),
        compiler_params=_cp(("parallel", "parallel", "parallel")),
        name="out_proj",
    )(x3, *mixes, w_out, mod3)


def _fnorm_kernel(x_ref, g_ref, o_ref):
    x = x_ref[...]
    ms = jnp.mean(x * x, axis=-1, keepdims=True)
    o_ref[...] = x * lax.rsqrt(ms + RMS_EPS) * g_ref[...]


def _final_norm(x3, gain):
    B, T, D = x3.shape
    M = B * T
    tm = _pick(M, 512, 8)
    y = pl.pallas_call(
        _fnorm_kernel,
        grid=(M // tm,),
        in_specs=[pl.BlockSpec((tm, D), lambda i: (i, 0)), pl.BlockSpec((1, D), lambda i: (0, 0))],
        out_specs=pl.BlockSpec((tm, D), lambda i: (i, 0)),
        out_shape=jax.ShapeDtypeStruct((M, D), F32),
        compiler_params=_cp(("parallel",)),
        name="final_norm",
    )(x3.reshape(M, D), gain.reshape(1, D))
    return y.reshape(B, T, D)


def _gla_head(q, k, v, gg, z, gn, st_scr, b_scr, k_scr, C, SB):
    nsb = C // SB
    q = q * (HEAD_DIM ** -0.5)
    b = _cumsum_rows(_log_sigmoid(z) * (1.0 / GLA_TAU))
    b_scr[...] = b
    k_scr[...] = k

    row1 = lax.broadcasted_iota(jnp.int32, (C, 1), 0)
    blk1 = row1 // SB
    rin1 = row1 - blk1 * SB
    rowc = lax.broadcasted_iota(jnp.int32, (C, C), 0)
    lanec = lax.broadcasted_iota(jnp.int32, (C, C), 1)
    blkc = rowc // SB

    def rows_of(ref, off):
        parts = [jnp.broadcast_to(ref[sb * SB + off:sb * SB + off + 1, :], (SB, HEAD_DIM)) for sb in range(nsb)]
        return parts[0] if nsb == 1 else jnp.concatenate(parts, axis=0)

    A = jnp.zeros((C, C), F32)
    for j in range(SB):
        bj = rows_of(b_scr, j)
        kj = rows_of(k_scr, j)
        e = jnp.exp(jnp.where(rin1 >= j, b - bj, -jnp.inf))
        col = jnp.sum(q * kj * e, axis=-1, keepdims=True)
        A = jnp.where(lanec == blkc * SB + j, col, A)

    if nsb > 1:
        r_own = rows_of(b_scr, SB - 1)
        ktil = k * jnp.exp(r_own - b)
        for sbj in range(nsb - 1):
            rj = b_scr[(sbj + 1) * SB - 1:(sbj + 1) * SB, :]
            qt = q * jnp.exp(jnp.where(blk1 > sbj, b - rj, 0.0))
            kt = jnp.where(blk1 == sbj, ktil, 0.0)
            aj = _dot_nt(qt.astype(BF16), kt.astype(BF16))
            A = A + jnp.where(blkc > sbj, aj, 0.0)

    st = st_scr[...]
    bl = b[C - 1:C, :]
    k2 = k * jnp.exp(bl - b)
    if C >= 16:
        o = jnp.dot(A.astype(BF16), v.astype(BF16), preferred_element_type=F32)
        kv = _dot_tn(v.astype(BF16), k2.astype(BF16))
    else:
        o = A[:, 0:1] * v[0:1, :]
        for s_ in range(1, C):
            o += A[:, s_:s_ + 1] * v[s_:s_ + 1, :]
        zpad = jnp.zeros((16 - C, HEAD_DIM), F32)
        kv = _dot_tn(jnp.concatenate([v, zpad], axis=0).astype(BF16), jnp.concatenate([k2, zpad], axis=0).astype(BF16))
    o += _dot_nt((q * jnp.exp(b)).astype(BF16), st.astype(BF16))
    st_scr[...] = st * jnp.exp(bl) + kv
    on = o * lax.rsqrt(jnp.mean(o * o, axis=-1, keepdims=True) + RMS_EPS) * gn
    return on * (gg * jax.nn.sigmoid(gg))


def _gla_kernel(q_ref, k_ref, v_ref, g_ref, gd_ref, gup_ref, gb_ref, gn_ref, s0_ref,
                o_ref, so_ref, st_scr, b_scr, k_scr, *, C, SB, HB):
    c = pl.program_id(2)

    @pl.when(c == 0)
    def _():
        for i in range(HB):
            st_scr[i] = s0_ref[i].T

    z_all = jnp.dot(gd_ref[...].astype(BF16), gup_ref[...], preferred_element_type=F32) + gb_ref[...]
    for i in range(HB):
        sl = slice(i * HEAD_DIM, (i + 1) * HEAD_DIM)
        out = _gla_head(q_ref[:, sl], k_ref[:, sl], v_ref[:, sl], g_ref[:, sl], z_all[:, sl], gn_ref[:, sl],
                        st_scr.at[i], b_scr.at[i], k_scr.at[i], C, SB)
        o_ref[:, sl] = out.astype(BF16)

    @pl.when(c == pl.num_programs(2) - 1)
    def _():
        for i in range(HB):
            so_ref[i] = st_scr[i].T


def _gla(proj, lay, B, T, gup_pad, gate_b, gla_norm, s0):
    GB = lay["GB"]
    H = GB
    if T % GLA_CHUNK == 0:
        C, SB = GLA_CHUNK, GLA_SUB
    else:
        C, SB = T, T
    nc = T // C
    HB = 4 if H % 4 == 0 else (2 if H % 2 == 0 else 1)
    W = HB * LANES

    def col(base):
        return pl.BlockSpec((C, W), lambda b, h, c: (b * nc + c, base // HB + h))

    mix, s_out = pl.pallas_call(
        functools.partial(_gla_kernel, C=C, SB=SB, HB=HB),
        grid=(B, H // HB, nc),
        in_specs=[col(0), col(GB), col(2 * GB), col(3 * GB),
                  pl.BlockSpec((C, LANES), lambda b, h, c: (b * nc + c, lay["SMALL_A"])),
                  pl.BlockSpec((LANES, W), lambda b, h, c: (0, h)),
                  pl.BlockSpec((1, W), lambda b, h, c: (0, h)),
                  pl.BlockSpec((1, W), lambda b, h, c: (0, h)),
                  pl.BlockSpec((None, HB, HEAD_DIM, HEAD_DIM), lambda b, h, c: (b, h, 0, 0))],
        out_specs=[pl.BlockSpec((C, W), lambda b, h, c: (b * nc + c, h)),
                   pl.BlockSpec((None, HB, HEAD_DIM, HEAD_DIM), lambda b, h, c: (b, h, 0, 0))],
        out_shape=[jax.ShapeDtypeStruct((B * T, H * HEAD_DIM), BF16),
                   jax.ShapeDtypeStruct((B, H, HEAD_DIM, HEAD_DIM), F32)],
        scratch_shapes=[pltpu.VMEM((HB, HEAD_DIM, HEAD_DIM), F32),
                        pltpu.VMEM((HB, C, HEAD_DIM), F32),
                        pltpu.VMEM((HB, C, HEAD_DIM), F32)],
        compiler_params=_cp(("parallel", "parallel", "arbitrary")),
        name="gla",
    )(proj, proj, proj, proj, proj, gup_pad, gate_b.reshape(1, -1), gla_norm.reshape(1, -1), s0)
    return mix, s_out


def _foxcum_kernel(x_ref, fb_ref, lf_ref, cum_ref, carry):
    @pl.when(pl.program_id(1) == 0)
    def _():
        carry[...] = jnp.zeros_like(carry)

    lf = _log_sigmoid(x_ref[...] + fb_ref[...])
    cs = _cumsum_rows(lf) + carry[...]
    lf_ref[...] = lf
    cum_ref[...] = cs
    n = cs.shape[0]
    carry[...] = cs[n - 1:n, :]


def _fox_cum(proj, lay, B, T, fb_row):
    tt = _pick(T, 256, 8)
    nt = T // tt
    spec = pl.BlockSpec((tt, LANES), lambda b, t: (b * nt + t, 0))
    return pl.pallas_call(
        _foxcum_kernel,
        grid=(B, nt),
        in_specs=[pl.BlockSpec((tt, LANES), lambda b, t: (b * nt + t, lay["SMALL_A"])),
                  pl.BlockSpec((1, LANES), lambda b, t: (0, 0))],
        out_specs=[spec, spec],
        out_shape=[jax.ShapeDtypeStruct((B * T, LANES), F32)] * 2,
        scratch_shapes=[pltpu.VMEM((1, LANES), F32)],
        compiler_params=_cp(("parallel", "arbitrary")),
        name="fox_cum",
    )(proj, fb_row)


def _per_query_block(i, nq, tq, T, body, align=1):
    for ii in range(nq):
        nk = min(T, -(-((ii + 1) * tq) // align) * align)
        pl.when(i == ii)(functools.partial(body, nk))


def _fox_prompt_kernel(q_ref, k_ref, v_ref, cq_ref, ck_ref, o_ref, *, tq):
    i = pl.program_id(2)
    T = k_ref.shape[0]

    def body(nk):
        q = (q_ref[...] * (HEAD_DIM ** -0.5)).astype(BF16)
        s = _dot_nt(q, k_ref[0:nk, :].astype(BF16))
        s = s + cq_ref[...] - ck_ref[:, 0:nk]
        row = i * tq + lax.broadcasted_iota(jnp.int32, (tq, nk), 0)
        colk = lax.broadcasted_iota(jnp.int32, (tq, nk), 1)
        s = jnp.where(colk <= row, s, -jnp.inf)
        m = jnp.max(s, axis=-1, keepdims=True)
        p = jnp.exp(s - m)
        l = jnp.sum(p, axis=-1, keepdims=True)
        o = jnp.dot(p.astype(BF16), v_ref[0:nk, :].astype(BF16), preferred_element_type=F32)
        o_ref[...] = (o / l).astype(BF16)

    _per_query_block(i, T // tq, tq, T, body)


def _fox_prompt(proj, lay, B, T, cum_col, cum_row):
    GB = lay["GB"]
    H = GB
    tq = _pick(T, 256, 8)
    nq = T // tq
    return pl.pallas_call(
        functools.partial(_fox_prompt_kernel, tq=tq),
        grid=(B, H, nq),
        in_specs=[pl.BlockSpec((tq, LANES), lambda b, h, i: (b * nq + i, 4 * GB + h)),
                  pl.BlockSpec((T, LANES), lambda b, h, i: (b, 5 * GB + h)),
                  pl.BlockSpec((T, LANES), lambda b, h, i: (b, 6 * GB + h)),
                  pl.BlockSpec((None, None, tq, 1), lambda b, h, i: (b, h, i, 0)),
                  pl.BlockSpec((None, None, 1, T), lambda b, h, i: (b, h, 0, 0))],
        out_specs=pl.BlockSpec((tq, LANES), lambda b, h, i: (b * nq + i, h)),
        out_shape=jax.ShapeDtypeStruct((B * T, H * HEAD_DIM), BF16),
        compiler_params=_cp(("parallel", "parallel", "parallel")),
        name="fox_prompt",
    )(proj, proj, proj, cum_col, cum_row)


def _rope_kernel(q_ref, k_ref, cos_ref, sin_ref, qo_ref, ko_ref):
    cos = cos_ref[...]
    sin = sin_ref[...]
    q = q_ref[...]
    k = k_ref[...]
    qo_ref[...] = q * cos + pltpu.roll(q, HEAD_DIM // 2, axis=1) * sin
    ko_ref[...] = k * cos + pltpu.roll(k, HEAD_DIM // 2, axis=1) * sin


def _rope(proj, lay, B, T, pos0):
    GB = lay["GB"]
    H = GB
    half = HEAD_DIM // 2
    inv = 1.0 / (ROPE_THETA ** (jnp.arange(0, HEAD_DIM, 2, dtype=F32) / HEAD_DIM))
    ang = (pos0 + jnp.arange(T, dtype=jnp.int32)).astype(F32)[:, None] * inv[None, :]
    cos = jnp.concatenate([jnp.cos(ang), jnp.cos(ang)], axis=-1)
    sin = jnp.concatenate([-jnp.sin(ang), jnp.sin(ang)], axis=-1)
    assert cos.shape == (T, 2 * half)
    tt = _pick(T, 512, 8)
    nt = T // tt
    ospec = pl.BlockSpec((tt, LANES), lambda b, t, h: (b * nt + t, h))
    tspec = pl.BlockSpec((tt, LANES), lambda b, t, h: (t, 0))
    return pl.pallas_call(
        _rope_kernel,
        grid=(B, nt, H),
        in_specs=[pl.BlockSpec((tt, LANES), lambda b, t, h: (b * nt + t, 7 * GB + h)),
                  pl.BlockSpec((tt, LANES), lambda b, t, h: (b * nt + t, 8 * GB + h)),
                  tspec, tspec],
        out_specs=[ospec, ospec],
        out_shape=[jax.ShapeDtypeStruct((B * T, H * HEAD_DIM), F32)] * 2,
        compiler_params=_cp(("parallel", "parallel", "parallel")),
        name="rope",
    )(proj, proj, cos, sin)


def _top_lanes(g, avail, n_pick):
    lane = lax.broadcasted_iota(jnp.int32, g.shape, g.ndim - 1)
    sel = jnp.zeros(g.shape, dtype=jnp.bool_)
    for _ in range(n_pick):
        cur = jnp.where(avail, g, -jnp.inf)
        m = jnp.max(cur, axis=-1, keepdims=True)
        first = jnp.min(jnp.where(avail & (cur == m), lane, 1 << 30), axis=-1, keepdims=True)
        pick = avail & (lane == first)
        sel = sel | pick
        avail = avail & jnp.logical_not(pick)
    return sel


def _moba_prompt_kernel(q_ref, k_ref, v_ref, o_ref, *, tq):
    i = pl.program_id(2)
    T = k_ref.shape[0]

    def body(nk):
        NB = nk // MOBA_BLOCK
        q = q_ref[...]
        k = k_ref[0:nk, :]
        kmean = jnp.mean(k.reshape(NB, MOBA_BLOCK, HEAD_DIM), axis=1)
        if NB < LANES:
            kmean = jnp.concatenate([kmean, jnp.zeros((LANES - NB, HEAD_DIM), F32)], axis=0)
        g = _dot_nt(q, kmean, precision=HIGHEST)
        row1 = i * tq + lax.broadcasted_iota(jnp.int32, (tq, 1), 0)
        own1 = row1 // MOBA_BLOCK
        blk = lax.broadcasted_iota(jnp.int32, (tq, LANES), 1)
        sel = _top_lanes(g, blk < own1, MOBA_TOPK)
        expand = (lax.broadcasted_iota(jnp.int32, (LANES, nk), 0)
                  == lax.broadcasted_iota(jnp.int32, (LANES, nk), 1) // MOBA_BLOCK).astype(BF16)
        selk = jnp.dot(sel.astype(BF16), expand, preferred_element_type=F32) > 0.5
        colk = lax.broadcasted_iota(jnp.int32, (tq, nk), 1)
        ok = selk | ((colk // MOBA_BLOCK == own1) & (colk <= row1))
        s = _dot_nt((q * (HEAD_DIM ** -0.5)).astype(BF16), k.astype(BF16))
        s = jnp.where(ok, s, -jnp.inf)
        m = jnp.max(s, axis=-1, keepdims=True)
        p = jnp.exp(s - m)
        l = jnp.sum(p, axis=-1, keepdims=True)
        o = jnp.dot(p.astype(BF16), v_ref[0:nk, :].astype(BF16), preferred_element_type=F32)
        o_ref[...] = (o / l).astype(BF16)

    _per_query_block(i, T // tq, tq, T, body, align=MOBA_BLOCK)


def _moba_prompt(mq, mk, proj, lay, B, T):
    GB = lay["GB"]
    H = GB
    assert T % MOBA_BLOCK == 0 and T // MOBA_BLOCK <= LANES
    tq = _pick(T, 256, 8)
    nq = T // tq
    return pl.pallas_call(
        functools.partial(_moba_prompt_kernel, tq=tq),
        grid=(B, H, nq),
        in_specs=[pl.BlockSpec((tq, LANES), lambda b, h, i: (b * nq + i, h)),
                  pl.BlockSpec((T, LANES), lambda b, h, i: (b, h)),
                  pl.BlockSpec((T, LANES), lambda b, h, i: (b, 9 * GB + h))],
        out_specs=pl.BlockSpec((tq, LANES), lambda b, h, i: (b * nq + i, h)),
        out_shape=jax.ShapeDtypeStruct((B * T, H * HEAD_DIM), BF16),
        compiler_params=_cp(("parallel", "parallel", "parallel")),
        name="moba_prompt",
    )(mq, mk, proj)


def _head_keys(page_refs, h, H):
    PAGE = page_refs[0].shape[0] // H
    parts = [r[pl.ds(h, PAGE, stride=H), :].astype(BF16) for r in page_refs]
    return parts[0] if len(parts) == 1 else jnp.concatenate(parts, axis=0)


def _paged_rows(cache):
    L, NPOOL, PAGE, H, hd = cache.shape
    rows = cache.reshape(L * NPOOL * PAGE * H, hd)

    def spec(layer, page_of):
        return pl.BlockSpec((PAGE * H, hd), lambda b, j, pt: (layer * NPOOL + page_of(b, j, pt), 0))

    return rows, spec


def _paged_step(H, qn_ref, k_refs, v_refs, adjust, m_scr, l_scr, acc_scr):
    nblk = len(k_refs) * (k_refs[0].shape[0] // H) // LANES
    logits = []
    for h in range(H):
        qb = (qn_ref[:, h * HEAD_DIM:(h + 1) * HEAD_DIM] * (HEAD_DIM ** -0.5)).astype(BF16)
        s = _dot_nt(qb, _head_keys(k_refs, h, H))
        logits.append([adjust(h, i, s[:, i * LANES:(i + 1) * LANES]) for i in range(nblk)])
    probs, alphas = [], []
    for h in range(H):
        mx = logits[h][0][0]
        for sb, _ in logits[h][1:]:
            mx = jnp.maximum(mx, sb)
        m_old = m_scr[h]
        m_new = jnp.maximum(m_old, jnp.max(mx, axis=-1, keepdims=True))
        alpha = jnp.exp(m_old - m_new)
        pb = [jnp.exp(sb - m_new) if ok is None else jnp.where(ok, jnp.exp(sb - m_new), 0.0) for sb, ok in logits[h]]
        tot = pb[0]
        for x in pb[1:]:
            tot = tot + x
        m_scr[h] = m_new
        l_scr[h] = alpha * l_scr[h] + jnp.sum(tot, axis=-1, keepdims=True)
        probs.append((pb[0] if nblk == 1 else jnp.concatenate(pb, axis=1)).astype(BF16))
        alphas.append(alpha)
    for h in range(H):
        acc_scr[h] = alphas[h] * acc_scr[h] + jnp.dot(probs[h], _head_keys(v_refs, h, H), preferred_element_type=F32)


def _new_token_columns(h, q, kn_ref, vn_ref, bias_col, bias_row_at, m, l, acc):
    T = q.shape[0]
    trow = lax.broadcasted_iota(jnp.int32, (T, 1), 0)
    cols = []
    for s_ in range(T):
        kn = kn_ref[s_:s_ + 1, h * HEAD_DIM:(h + 1) * HEAD_DIM]
        c = jnp.sum(q * kn, axis=-1, keepdims=True) + bias_col - bias_row_at(s_)
        cols.append(jnp.where(trow >= s_, c, NEG))
    m_new = m
    for c in cols:
        m_new = jnp.maximum(m_new, c)
    alpha = jnp.exp(m - m_new)
    l = alpha * l
    acc = alpha * acc
    for s_, c in enumerate(cols):
        p = jnp.where(trow >= s_, jnp.exp(c - m_new), 0.0)
        l = l + p
        acc = acc + p * vn_ref[s_:s_ + 1, h * HEAD_DIM:(h + 1) * HEAD_DIM]
    return l, acc


def _fox_sample_kernel(pt_ref, *refs, H, PP):
    del pt_ref
    k_refs = refs[0:PP]
    v_refs = refs[PP:2 * PP]
    lf_refs = refs[2 * PP:3 * PP]
    qn_ref, kn_ref, vn_ref, cc_ref, cr_ref, o_ref, m_scr, l_scr, acc_scr, carry = refs[3 * PP:]
    j = pl.program_id(1)
    T = qn_ref.shape[0]
    PAGE = k_refs[0].shape[0] // H

    @pl.when(j == 0)
    def _():
        m_scr[...] = jnp.full(m_scr.shape, NEG, F32)
        l_scr[...] = jnp.zeros_like(l_scr)
        acc_scr[...] = jnp.zeros_like(acc_scr)
        carry[...] = jnp.zeros_like(carry)

    lane = lax.broadcasted_iota(jnp.int32, (H, PAGE), 1)
    biases = []
    run = carry[...]
    for i in range(PP):
        x = lf_refs[i][...]
        y = x
        s_ = 1
        while s_ < PAGE:
            y = y + jnp.where(lane + s_ < PAGE, pltpu.roll(y, PAGE - s_, axis=1), 0.0)
            s_ *= 2
        biases.append(y - x + run)
        run = run + jnp.broadcast_to(y[:, 0:1], (H, PAGE))
    carry[...] = run

    assert PAGE == LANES
    _paged_step(H, qn_ref, k_refs, v_refs,
                lambda h, i, sb: (sb + biases[i][h:h + 1, :] + cc_ref[h], None),
                m_scr, l_scr, acc_scr)

    @pl.when(j == pl.num_programs(1) - 1)
    def _():
        for h in range(H):
            q = qn_ref[:, h * HEAD_DIM:(h + 1) * HEAD_DIM] * (HEAD_DIM ** -0.5)
            l, acc = _new_token_columns(
                h, q, kn_ref, vn_ref, cc_ref[h], lambda s_: cr_ref[h:h + 1, s_:s_ + 1],
                m_scr[h], l_scr[h], acc_scr[h])
            o_ref[:, h * HEAD_DIM:(h + 1) * HEAD_DIM] = (acc / l).astype(BF16)


def _fox_sample(proj, lay, DB, T, layer, cache_k, cache_v, cache_lf_t, page_table, cum_cols, cum_rows):
    GB = lay["GB"]
    H = GB
    G = H * HEAD_DIM
    PAGE = cache_k.shape[2]
    n_pages = page_table.shape[1]
    PP = PAGES_PER_STEP if n_pages % PAGES_PER_STEP == 0 else 1
    NS = n_pages // PP
    k_rows, page_spec = _paged_rows(cache_k)
    v_rows, _ = _paged_rows(cache_v)

    def kv_spec(i):
        return page_spec(layer, lambda b, j, pt: pt[b, n_pages - 1 - (j * PP + i)])

    def lf_spec(i):
        return pl.BlockSpec((None, None, H, PAGE),
                            lambda b, j, pt: (layer, pt[b, n_pages - 1 - (j * PP + i)], 0, 0))

    def new_spec(idx):
        return pl.BlockSpec((T, G), lambda b, j, pt: (b, idx))

    grid_spec = pltpu.PrefetchScalarGridSpec(
        num_scalar_prefetch=1,
        grid=(DB, NS),
        in_specs=([kv_spec(i) for i in range(PP)] + [kv_spec(i) for i in range(PP)]
                  + [lf_spec(i) for i in range(PP)]
                  + [new_spec(4), new_spec(5), new_spec(6),
                     pl.BlockSpec((None, H, T, LANES), lambda b, j, pt: (b, 0, 0, 0)),
                     pl.BlockSpec((None, H, T), lambda b, j, pt: (b, 0, 0))]),
        out_specs=pl.BlockSpec((T, G), lambda b, j, pt: (b, 0)),
        scratch_shapes=[pltpu.VMEM((H, T, LANES), F32), pltpu.VMEM((H, T, LANES), F32),
                        pltpu.VMEM((H, T, HEAD_DIM), F32), pltpu.VMEM((H, PAGE), F32)],
    )
    return pl.pallas_call(
        functools.partial(_fox_sample_kernel, H=H, PP=PP),
        grid_spec=grid_spec,
        out_shape=jax.ShapeDtypeStruct((DB * T, G), BF16),
        compiler_params=_cp(("parallel", "arbitrary")),
        name="fox_sample",
    )(page_table, *([k_rows] * PP), *([v_rows] * PP), *([cache_lf_t] * PP),
      proj, proj, proj, cum_cols, cum_rows)


def _moba_sel_kernel(pt_ref, *refs, H, PP, PPB):
    del pt_ref
    k_refs = refs[0:PP]
    q_ref, sel_ref, g_scr = refs[PP:]
    j = pl.program_id(1)
    T = q_ref.shape[0]
    PAGE = k_refs[0].shape[0]

    @pl.when(j == 0)
    def _():
        g_scr[...] = jnp.zeros_like(g_scr)

    lane = lax.broadcasted_iota(jnp.int32, (H, LANES), 1)
    for bi in range(PP // PPB):
        ks = jnp.sum(k_refs[bi * PPB][...], axis=0)
        for r in range(1, PPB):
            ks = ks + jnp.sum(k_refs[bi * PPB + r][...], axis=0)
        kmean = ks * (1.0 / (PPB * PAGE))
        blk = j * (PP // PPB) + bi
        for t in range(T):
            gcol = jnp.sum(q_ref[t] * kmean, axis=-1, keepdims=True)
            g_scr[t] = jnp.where(lane == blk, gcol, g_scr[t])

    @pl.when(j == pl.num_programs(1) - 1)
    def _():
        n_blk = pl.num_programs(1) * (PP // PPB)
        for t in range(T):
            sel = _top_lanes(g_scr[t], lane < n_blk, MOBA_TOPK)
            sel_ref[t] = sel.astype(F32)


def _moba_sel(mq3, layer, cache_k, page_table):
    DB, T, H, _ = mq3.shape
    PAGE = cache_k.shape[2]
    n_pages = page_table.shape[1]
    PPB = MOBA_BLOCK // PAGE
    PP = PAGES_PER_STEP if n_pages % PAGES_PER_STEP == 0 else PPB
    assert MOBA_BLOCK % PAGE == 0 and PP % PPB == 0 and n_pages % PP == 0 and n_pages // PPB <= LANES
    NS = n_pages // PP

    def k_spec(i):
        return pl.BlockSpec((None, None, PAGE, H, HEAD_DIM),
                            lambda b, j, pt: (layer, pt[b, j * PP + i], 0, 0, 0))

    grid_spec = pltpu.PrefetchScalarGridSpec(
        num_scalar_prefetch=1,
        grid=(DB, NS),
        in_specs=[k_spec(i) for i in range(PP)]
                 + [pl.BlockSpec((None, T, H, HEAD_DIM), lambda b, j, pt: (b, 0, 0, 0))],
        out_specs=pl.BlockSpec((None, T, H, LANES), lambda b, j, pt: (b, 0, 0, 0)),
        scratch_shapes=[pltpu.VMEM((T, H, LANES), F32)],
    )
    return pl.pallas_call(
        functools.partial(_moba_sel_kernel, H=H, PP=PP, PPB=PPB),
        grid_spec=grid_spec,
        out_shape=jax.ShapeDtypeStruct((DB, T, H, LANES), F32),
        compiler_params=_cp(("parallel", "arbitrary")),
        name="moba_sel",
    )(page_table, *([cache_k] * PP), mq3)


def _moba_sample_kernel(pt_ref, *refs, H, PP, PPB):
    del pt_ref
    k_refs = refs[0:PP]
    v_refs = refs[PP:2 * PP]
    qn_ref, kn_ref, vn_ref, sel_ref, o_ref, m_scr, l_scr, acc_scr = refs[2 * PP:]
    j = pl.program_id(1)
    T = qn_ref.shape[0]
    PAGE = k_refs[0].shape[0] // H
    W = PP * PAGE

    @pl.when(j == 0)
    def _():
        m_scr[...] = jnp.full(m_scr.shape, NEG, F32)
        l_scr[...] = jnp.zeros_like(l_scr)
        acc_scr[...] = jnp.zeros_like(acc_scr)

    expand = (lax.broadcasted_iota(jnp.int32, (LANES, W), 0)
              == j * (PP // PPB) + lax.broadcasted_iota(jnp.int32, (LANES, W), 1) // (PPB * PAGE)).astype(BF16)
    oks = [jnp.dot(sel_ref[h].astype(BF16), expand, preferred_element_type=F32) > 0.5 for h in range(H)]

    def adjust(h, i, sb):
        ok = oks[h][:, i * LANES:(i + 1) * LANES]
        return jnp.where(ok, sb, NEG), ok

    _paged_step(H, qn_ref, k_refs, v_refs, adjust, m_scr, l_scr, acc_scr)

    @pl.when(j == pl.num_programs(1) - 1)
    def _():
        zero = jnp.zeros((1, 1), F32)
        for h in range(H):
            q = qn_ref[:, h * HEAD_DIM:(h + 1) * HEAD_DIM] * (HEAD_DIM ** -0.5)
            l, acc = _new_token_columns(h, q, kn_ref, vn_ref, zero, lambda s_: zero,
                                        m_scr[h], l_scr[h], acc_scr[h])
            o_ref[:, h * HEAD_DIM:(h + 1) * HEAD_DIM] = (acc / l).astype(BF16)


def _moba_sample(mq, mk, proj, lay, DB, T, layer, cache_k, cache_v, page_table, sel):
    GB = lay["GB"]
    H = GB
    G = H * HEAD_DIM
    PAGE = cache_k.shape[2]
    n_pages = page_table.shape[1]
    PPB = MOBA_BLOCK // PAGE
    PP = PAGES_PER_STEP if n_pages % PAGES_PER_STEP == 0 else PPB
    NS = n_pages // PP
    assert (n_pages * PAGE) % MOBA_BLOCK == 0 and T <= MOBA_BLOCK

    k_rows, page_spec = _paged_rows(cache_k)
    v_rows, _ = _paged_rows(cache_v)

    def kv_spec(i):
        return page_spec(layer, lambda b, j, pt: pt[b, j * PP + i])

    grid_spec = pltpu.PrefetchScalarGridSpec(
        num_scalar_prefetch=1,
        grid=(DB, NS),
        in_specs=([kv_spec(i) for i in range(PP)] + [kv_spec(i) for i in range(PP)]
                  + [pl.BlockSpec((T, G), lambda b, j, pt: (b, 0)),
                     pl.BlockSpec((T, G), lambda b, j, pt: (b, 0)),
                     pl.BlockSpec((T, G), lambda b, j, pt: (b, 9)),
                     pl.BlockSpec((None, H, T, LANES), lambda b, j, pt: (b, 0, 0, 0))]),
        out_specs=pl.BlockSpec((T, G), lambda b, j, pt: (b, 0)),
        scratch_shapes=[pltpu.VMEM((H, T, LANES), F32), pltpu.VMEM((H, T, LANES), F32),
                        pltpu.VMEM((H, T, HEAD_DIM), F32)],
    )
    return pl.pallas_call(
        functools.partial(_moba_sample_kernel, H=H, PP=PP, PPB=PPB),
        grid_spec=grid_spec,
        out_shape=jax.ShapeDtypeStruct((DB * T, G), BF16),
        compiler_params=_cp(("parallel", "arbitrary")),
        name="moba_sample",
    )(page_table, *([k_rows] * PP), *([v_rows] * PP), mq, mk, proj, sel)


RW_R, RW_W, RW_K, RW_KK, RW_KKA, RW_WR, RW_N = 0, 1, 2, 3, 4, 5, 6


def _seg_sum(x):
    bd = (lax.broadcasted_iota(jnp.int32, (LANES, LANES), 0) // RWKV_HEAD
          == lax.broadcasted_iota(jnp.int32, (LANES, LANES), 1) // RWKV_HEAD).astype(F32)
    outs = [jnp.dot(x[:, c * LANES:(c + 1) * LANES], bd, precision=HIGHEST, preferred_element_type=F32)
            for c in range(x.shape[1] // LANES)]
    return outs[0] if len(outs) == 1 else jnp.concatenate(outs, axis=1)


def _rwkv_prep_kernel(r_ref, k_ref, v_ref, d_ref, rp_ref, kp_ref, vp_ref, dp_ref,
                      s0m_ref, s0d_ref, mum_ref, mud_ref,
                      wup_ref, aup_ref, gup_ref, w0_ref, a0_ref, kk_ref, ka_ref, rk_ref,
                      kvo, vo, go, bo):
    t = pl.program_id(1)
    tt, G = r_ref.shape
    first = t == 0

    def mixed(cur_ref, prev_ref, s0, mu):
        cur = cur_ref[...]
        n = cur.shape[0]
        prow = jnp.where(first, s0, prev_ref[prev_ref.shape[0] - 1:, :])
        row = lax.broadcasted_iota(jnp.int32, cur.shape, 0)
        prev = jnp.where(row == 0, prow, pltpu.roll(cur, 1, axis=0)) if n > 1 else prow
        return cur + (prev - cur) * mu

    r = mixed(r_ref, rp_ref, s0m_ref[:, 0:G], mum_ref[:, 0:G])
    kr = mixed(k_ref, kp_ref, s0m_ref[:, G:2 * G], mum_ref[:, G:2 * G])
    vr = mixed(v_ref, vp_ref, s0m_ref[:, 2 * G:3 * G], mum_ref[:, 2 * G:3 * G])
    d = mixed(d_ref, dp_ref, s0d_ref[...], mud_ref[...])

    dwa = d[:, 0:LANES]
    tw = jnp.dot(jnp.tanh(dwa).astype(BF16), wup_ref[...], preferred_element_type=F32)
    w_raw = -_softplus(-(w0_ref[...] + tw)) - 0.5
    w = jnp.exp(-jnp.exp(w_raw))
    a = jax.nn.sigmoid(a0_ref[...] + jnp.dot(dwa.astype(BF16), aup_ref[...], preferred_element_type=F32))
    g = jnp.dot(jax.nn.sigmoid(d[:, LANES:]).astype(BF16), gup_ref[...], preferred_element_type=F32)
    kk = kr * kk_ref[...]
    kk = kk / jnp.maximum(jnp.sqrt(_seg_sum(kk * kk)), 1e-12)
    k2 = kr * (1.0 + (a - 1.0) * ka_ref[...])
    kvo[RW_R] = r
    kvo[RW_W] = w
    kvo[RW_K] = k2
    kvo[RW_KK] = kk
    kvo[RW_KKA] = kk * a
    kvo[RW_WR] = w * r
    vo[...] = vr
    go[...] = g
    bo[...] = _seg_sum(r * k2 * rk_ref[...]) * vr


def _rwkv_prep(proj, lay, B, T, shift_main, shift_down, mu_main, mu_down, wts):
    GB = lay["GB"]
    G = GB * LANES
    tt = _pick(T, 256, 8)
    nt = T // tt
    dcol = lay["DOWN"]
    assert (dcol * LANES) % (3 * LANES) == 0
    dblk = dcol // 3

    def cur(idx, w):
        return pl.BlockSpec((tt, w), lambda b, t: (b * nt + t, idx))

    def prev(idx, w):
        return pl.BlockSpec((8, w), lambda b, t: (jnp.maximum((b * T + t * tt) // 8 - 1, 0), idx))

    full = lambda a: pl.BlockSpec(a.shape, lambda b, t: (0,) * a.ndim)
    ospec = pl.BlockSpec((tt, G), lambda b, t: (b * nt + t, 0))
    outs = pl.pallas_call(
        _rwkv_prep_kernel,
        grid=(B, nt),
        in_specs=[cur(10, G), cur(11, G), cur(12, G), cur(dblk, 3 * LANES),
                  prev(10, G), prev(11, G), prev(12, G), prev(dblk, 3 * LANES),
                  pl.BlockSpec((None, 1, 3 * G), lambda b, t: (b, 0, 0)),
                  pl.BlockSpec((None, 1, 3 * LANES), lambda b, t: (b, 0, 0)),
                  full(mu_main), full(mu_down)] + [full(a) for a in wts],
        out_specs=[pl.BlockSpec((RW_N, tt, G), lambda b, t: (0, b * nt + t, 0))] + [ospec] * 3,
        out_shape=[jax.ShapeDtypeStruct((RW_N, B * T, G), F32)] + [jax.ShapeDtypeStruct((B * T, G), F32)] * 3,
        compiler_params=_cp(("parallel", "arbitrary")),
        name="rwkv_prep",
    )(proj, proj, proj, proj, proj, proj, proj, proj, shift_main, shift_down, mu_main, mu_down, *wts)
    return outs


def _rwkv_scan_kernel(kv_ref, v_ref, s0_ref, y_ref, so_ref, s_scr):
    i = pl.program_id(0)
    _, tt, K, L = kv_ref.shape

    @pl.when(i == 0)
    def _():
        s_scr[...] = s0_ref[...]

    def row(a, t, kidx):
        return kv_ref[a, t, pl.ds(kidx, 1), :]

    vshape = v_ref.shape[1:]
    sa0 = jnp.zeros(vshape, F32)
    ya0 = jnp.zeros(vshape, F32)
    for kidx in range(K):
        s = s_scr[kidx]
        sa0 = sa0 - s * row(RW_KK, 0, kidx)
        ya0 = ya0 + s * row(RW_WR, 0, kidx)

    def token(t, carry):
        sa, ya = carry
        vv = v_ref[t]
        rt = kv_ref[RW_R, t]
        kr_dot = jnp.sum(kv_ref[RW_K, t] * rt, axis=0, keepdims=True)
        kar_dot = jnp.sum(kv_ref[RW_KKA, t] * rt, axis=0, keepdims=True)
        y_ref[t] = ya + sa * kar_dot + vv * kr_dot
        tn = jnp.minimum(t + 1, tt - 1)
        sa_n = jnp.zeros(vshape, F32)
        ya_n = jnp.zeros(vshape, F32)
        for kidx in range(K):
            s = s_scr[kidx] * row(RW_W, t, kidx) + sa * row(RW_KKA, t, kidx) + vv * row(RW_K, t, kidx)
            s_scr[kidx] = s
            sa_n = sa_n - s * row(RW_KK, tn, kidx)
            ya_n = ya_n + s * row(RW_WR, tn, kidx)
        return sa_n, ya_n

    lax.fori_loop(0, tt, token, (sa0, ya0))

    @pl.when(i == pl.num_programs(0) - 1)
    def _():
        so_ref[...] = s_scr[...]


def _rwkv_scan(kv, v, s0):
    _, T, K, L = kv.shape
    VP = v.shape[1]
    tt = _pick(T, 32, 1)
    vspec = pl.BlockSpec((tt, VP, L), lambda i: (i, 0, 0))
    sspec = pl.BlockSpec((K, VP, L), lambda i: (0, 0, 0))
    return pl.pallas_call(
        _rwkv_scan_kernel,
        grid=(T // tt,),
        in_specs=[pl.BlockSpec((RW_N, tt, K, L), lambda i: (0, i, 0, 0)), vspec, sspec],
        out_specs=[vspec, sspec],
        out_shape=[jax.ShapeDtypeStruct((T, VP, L), F32), jax.ShapeDtypeStruct((K, VP, L), F32)],
        scratch_shapes=[pltpu.VMEM((K, VP, L), F32)],
        compiler_params=_cp(("arbitrary",)),
        name="rwkv_scan",
    )(kv, v, s0)


def _rwkv_post_kernel(y_ref, b_ref, g_ref, lw_ref, lb_ref, o_ref):
    y = y_ref[...]
    mu = _seg_sum(y) * (1.0 / RWKV_HEAD)
    yc = y - mu
    var = _seg_sum(yc * yc) * (1.0 / RWKV_HEAD)
    yn = yc * lax.rsqrt(var + RWKV_LN_EPS) * lw_ref[...] + lb_ref[...]
    o_ref[...] = ((yn + b_ref[...]) * g_ref[...]).astype(BF16)


def _rwkv_post(y, bonus, g, ln_w, ln_b):
    M, G = y.shape
    tm = _pick(M, 512, 8)
    spec = pl.BlockSpec((tm, G), lambda i: (i, 0))
    rspec = pl.BlockSpec((1, G), lambda i: (0, 0))
    return pl.pallas_call(
        _rwkv_post_kernel,
        grid=(M // tm,),
        in_specs=[spec, spec, spec, rspec, rspec],
        out_specs=spec,
        out_shape=jax.ShapeDtypeStruct((M, G), BF16),
        compiler_params=_cp(("parallel",)),
        name="rwkv_post",
    )(y, bonus, g, ln_w.reshape(1, G), ln_b.reshape(1, G))


def _rwkv(proj, lay, B, T, shift0, state0, p, l):
    GB = lay["GB"]
    G = GB * LANES
    RH = G // RWKV_HEAD
    N = RWKV_HEAD
    chains = B * RH
    assert LANES % chains == 0 or chains % LANES == 0
    VH = max(1, LANES // chains)
    VP = N // VH
    rw = p["rwkv_w_up"][l].shape[0]
    ra = p["rwkv_a_up"][l].shape[0]
    rg = p["rwkv_g_up"][l].shape[0]
    assert rw + ra == LANES and rg <= 2 * LANES

    def padrows(a, top, total):
        return jnp.pad(a, ((top, total - top - a.shape[0]), (0, 0))).astype(BF16)

    def pad_down(a):
        return jnp.pad(a, [(0, 0)] * (a.ndim - 1) + [(0, 3 * LANES - a.shape[-1])])

    wts = (padrows(p["rwkv_w_up"][l], 0, LANES), padrows(p["rwkv_a_up"][l], rw, LANES),
           padrows(p["rwkv_g_up"][l], 0, 2 * LANES),
           p["rwkv_w0"][l].reshape(1, G), p["rwkv_a0"][l].reshape(1, G),
           p["rwkv_k_k"][l].reshape(1, G), p["rwkv_k_a"][l].reshape(1, G), p["rwkv_r_k"][l].reshape(1, G))
    mu = p["rwkv_mu"][l]
    kv, v, g, bonus = _rwkv_prep(
        proj, lay, B, T,
        shift0[:, None, :3 * G], pad_down(shift0[:, None, 3 * G:]),
        mu[None, :3 * G], pad_down(mu[None, 3 * G:]), wts)

    kv_t = jnp.broadcast_to(kv.reshape(RW_N, 1, B, T, RH, N), (RW_N, VH, B, T, RH, N))
    kv_t = kv_t.transpose(0, 3, 5, 1, 2, 4).reshape(RW_N, T, N, VH * chains)
    v_t = v.reshape(B, T, RH, VH, VP).transpose(1, 4, 3, 0, 2).reshape(T, VP, VH * chains)
    s0_t = state0.reshape(B, RH, VH, VP, N).transpose(4, 3, 2, 0, 1).reshape(N, VP, VH * chains)
    y_t, s_t = _rwkv_scan(kv_t, v_t, s0_t)
    y = y_t.reshape(T, VP, VH, B, RH).transpose(3, 0, 4, 2, 1).reshape(B * T, G)
    s_new = s_t.reshape(N, VP, VH, B, RH).transpose(3, 4, 2, 1, 0).reshape(B, RH, N, N)
    mix = _rwkv_post(y, bonus, g, p["rwkv_ln_w"][l], p["rwkv_ln_b"][l])
    return mix, s_new


def _peer_pairs():
    return [(a, PEER_TOPK // (a + 1)) for a in range(PEER_TOPK)]


def _peer_topk_kernel(q_ref, keys_ref, e_ref, g_ref, v1_scr, i1_scr, v2_scr, i2_scr, c_scr, ci_scr, t_scr, te_scr):
    NK = keys_ref.shape[2]
    KH = keys_ref.shape[3]
    tn = q_ref.shape[0]
    TK = PEER_TOPK
    big = jnp.int32(1 << 30)

    def top_rows(s, iota, vals_scr, idx_scr, payload=None):
        for kk in range(TK):
            m = jnp.max(s, axis=0, keepdims=True)
            first = jnp.min(jnp.where(s == m, iota, big), axis=0, keepdims=True)
            hit = iota == first
            vals_scr[kk:kk + 1, :] = m
            if payload is None:
                idx_scr[kk:kk + 1, :] = first
            else:
                idx_scr[kk:kk + 1, :] = jnp.max(jnp.where(hit, payload, -1), axis=0, keepdims=True)
            s = jnp.where(hit, -jnp.inf, s)

    iota_k = lax.broadcasted_iota(jnp.int32, (NK, tn), 0)
    ncp = c_scr.shape[0]
    iota_c = lax.broadcasted_iota(jnp.int32, (ncp, tn), 0)
    for h in range(PEER_HEADS):
        for half, (vs, is_) in enumerate(((v1_scr, i1_scr), (v2_scr, i2_scr))):
            c0 = (h * 2 + half) * KH
            s = _dot_nt(keys_ref[h, half], q_ref[:, c0:c0 + KH], precision=HIGHEST)
            top_rows(s, iota_k, vs, is_)
        v2 = v2_scr[...]
        i2 = i2_scr[...]
        c_scr[ncp - 8:ncp, :] = jnp.full((8, tn), -jnp.inf, F32)
        ci_scr[ncp - 8:ncp, :] = jnp.zeros((8, tn), jnp.int32)
        off = 0
        for a, n in _peer_pairs():
            c_scr[off:off + n, :] = v1_scr[a:a + 1, :] + v2[0:n, :]
            ci_scr[off:off + n, :] = i1_scr[a:a + 1, :] * NK + i2[0:n, :]
            off += n
        top_rows(c_scr[...], iota_c, t_scr, te_scr, payload=ci_scr[...])
        top = t_scr[...]
        ex = jnp.exp(top - top[0:1, :])
        g_ref[h * TK:(h + 1) * TK, :] = ex / jnp.sum(ex, axis=0, keepdims=True)
        e_ref[h * TK:(h + 1) * TK, :] = te_scr[...]


def _peer_topk(qp, keys):
    M = qp.shape[0]
    NK = keys.shape[2]
    tn = _pick(M, 128)
    J = PEER_HEADS * PEER_TOPK
    TK = PEER_TOPK
    ncp = -(-sum(n for _, n in _peer_pairs()) // 8) * 8
    return pl.pallas_call(
        _peer_topk_kernel,
        grid=(M // tn,),
        in_specs=[pl.BlockSpec((tn, qp.shape[1]), lambda i: (i, 0)),
                  pl.BlockSpec(keys.shape, lambda i: (0, 0, 0, 0))],
        out_specs=[pl.BlockSpec((J, tn), lambda i: (0, i))] * 2,
        out_shape=[jax.ShapeDtypeStruct((J, M), jnp.int32), jax.ShapeDtypeStruct((J, M), F32)],
        scratch_shapes=[pltpu.VMEM((TK, tn), F32), pltpu.VMEM((TK, tn), jnp.int32),
                        pltpu.VMEM((TK, tn), F32), pltpu.VMEM((TK, tn), jnp.int32),
                        pltpu.VMEM((ncp, tn), F32), pltpu.VMEM((ncp, tn), jnp.int32),
                        pltpu.VMEM((TK, tn), F32), pltpu.VMEM((TK, tn), jnp.int32)],
        compiler_params=_cp(("parallel",)),
        name="peer_topk",
    )(qp, keys)


def _peer_w_kernel(e0_ref, g0_ref, e1_ref, g1_ref, w_ref, *, NK):
    tp, J = e0_ref.shape
    iota = lax.broadcasted_iota(jnp.int32, (NK, J), 0)
    shift = NK.bit_length() - 1
    assert 1 << shift == NK
    sub = lax.broadcasted_iota(jnp.int32, (NK // 8, 8, NK), 1)

    def table(e_ref, g_ref, n):
        e = e_ref[pl.ds(n, 1), :]
        g = g_ref[pl.ds(n, 1), :]
        a = e >> shift
        b = e & (NK - 1)
        at = jnp.where(iota == a, g, 0.0).astype(BF16)
        bt = (iota == b).astype(BF16)
        return _dot_nt(at, bt).reshape(NK // 8, 8, NK)

    def pair(m, carry):
        w0 = table(e0_ref, g0_ref, m)
        w1 = table(e1_ref, g1_ref, m)
        lo = jnp.where(sub < 4, w0, pltpu.roll(w1, 4, axis=1))
        hi = jnp.where(sub < 4, pltpu.roll(w0, 4, axis=1), w1)
        w_ref[:, pl.ds(pl.multiple_of(m * 8, 8), 8), :] = jnp.stack([lo, hi], axis=1).reshape(NK // 4, 8, NK)
        return carry

    lax.fori_loop(0, tp, pair, 0, unroll=4)


def _peer_w(eidx, gates, NK, tw):
    M, J = eidx.shape
    half = tw // 2
    tp = _pick(half, 32, 8)
    assert NK % 8 == 0 and tw % 2 == 0 and M % tw == 0
    nc = half // tp
    first = pl.BlockSpec((tp, J), lambda i, c: (i * 2 * nc + c, 0))
    second = pl.BlockSpec((tp, J), lambda i, c: (i * 2 * nc + nc + c, 0))
    return pl.pallas_call(
        functools.partial(_peer_w_kernel, NK=NK),
        grid=(M // tw, nc),
        in_specs=[first, first, second, second],
        out_specs=pl.BlockSpec((NK // 4, tp * 8, NK), lambda i, c: (0, i * nc + c, 0)),
        out_shape=jax.ShapeDtypeStruct((NK // 4, M // 2 * 8, NK), F32),
        compiler_params=_cp(("parallel", "parallel")),
        name="peer_w",
    )(eidx, gates, eidx, gates)


def _peer_dense_kernel(h_ref, u_ref, w_ref, v_ref, x_ref, ga_ref, o_ref):
    e = pl.program_id(2)

    @pl.when(e == 0)
    def _():
        o_ref[...] = jnp.zeros_like(o_ref)

    a = _dot_nt(h_ref[...], u_ref[...])
    act = 0.5 * a * (1.0 + lax.erf(a * (2.0 ** -0.5)))
    NK = w_ref.shape[1]
    half = w_ref.shape[0] // 8
    zs = []
    for r in range(4):
        wr = jnp.concatenate([w_ref[pl.ds(r, half, stride=8), :], w_ref[pl.ds(4 + r, half, stride=8), :]], axis=0)
        zs.append((act[:, r * NK:(r + 1) * NK] * wr).astype(BF16))
    z = jnp.concatenate(zs, axis=1)
    o_ref[...] += jnp.dot(z, v_ref[...], preferred_element_type=F32).reshape(o_ref.shape)

    @pl.when(e == pl.num_programs(2) - 1)
    def _():
        o_ref[...] = x_ref[...] + ga_ref[...] * o_ref[...]


def _peer_dense(x3, h2, w4, u, v, mod3, ga_idx):
    B, T, D = x3.shape
    E = u.shape[0]
    bb, tt = _row_tiles(B, T)
    NK = w4.shape[2]
    te = 4 * NK
    nt = T // tt
    half = bb * tt // 2
    assert w4.shape == (NK // 4, B * T * 4, NK) and E == NK * NK
    once = pl.Buffered(1)
    return pl.pallas_call(
        _peer_dense_kernel,
        grid=(B // bb, nt, E // te),
        in_specs=[pl.BlockSpec((bb * tt, D), lambda b, t, e: (b * nt + t, 0), pipeline_mode=once),
                  pl.BlockSpec((te, D), lambda b, t, e: (e, 0)),
                  pl.BlockSpec((None, half * 8, NK), lambda b, t, e: (e, b * nt + t, 0)),
                  pl.BlockSpec((te, D), lambda b, t, e: (e, 0)),
                  pl.BlockSpec((bb, tt, D), lambda b, t, e: (b, t, 0), pipeline_mode=once),
                  pl.BlockSpec((bb, 1, D), lambda b, t, e: (b, 0, ga_idx))],
        out_specs=pl.BlockSpec((bb, tt, D), lambda b, t, e: (b, t, 0)),
        out_shape=jax.ShapeDtypeStruct((B, T, D), F32),
        compiler_params=_cp(("parallel", "parallel", "arbitrary")),
        name="peer_dense",
    )(h2, u, w4, v, x3, mod3)


def _layout(G):
    GB = G // LANES
    return {"GB": GB, "SMALL_A": 13 * GB, "DOWN": 13 * GB + 1, "NP": (13 * GB + 4) * LANES}


def _pack_w_in(w_in, G, gate_rank, n_fox_heads, n_down):
    gla, fox = 0, 4 * G + gate_rank
    moba = fox + 3 * G + n_fox_heads
    rw = moba + 3 * G
    D = w_in.shape[0]
    zeros = lambda n: jnp.zeros((D, n), w_in.dtype)
    parts = [w_in[:, gla:gla + 4 * G], w_in[:, fox:fox + 3 * G], w_in[:, moba:moba + 3 * G], w_in[:, rw:rw + 3 * G],
             w_in[:, 4 * G:4 * G + gate_rank], w_in[:, fox + 3 * G:fox + 3 * G + n_fox_heads],
             zeros(LANES - gate_rank - n_fox_heads),
             w_in[:, rw + 3 * G:rw + 3 * G + n_down], zeros(3 * LANES - n_down)]
    return jnp.concatenate(parts, axis=1).astype(BF16)


def _layer(x3, mod3, l, p, wl, past, page_table):
    B, T, D = x3.shape
    G = D // 4
    lay = _layout(G)
    GB = lay["GB"]
    H = GB
    gate_rank = p["gla_gate_up"].shape[1]
    M = B * T

    proj = _norm_mod_matmul(x3, mod3, 1, 0, p["norm_mix"][l], wl["w_in"], emit_h=False, name="in_proj")

    mix_gla, gla_s = _gla(proj, lay, B, T, wl["gup"], p["gla_gate_b"][l], p["gla_norm"][l], past["gla"])

    lf, cum = _fox_cum(proj, lay, B, T, wl["fb_row"])
    lf8 = lf[:, gate_rank:gate_rank + H].reshape(B, T, H)
    cum8 = cum[:, gate_rank:gate_rank + H].reshape(B, T, H)
    if page_table is None:
        mix_fox = _fox_prompt(proj, lay, B, T, cum8.transpose(0, 2, 1)[..., None], cum8.transpose(0, 2, 1)[:, :, None, :])
    else:
        cum_rows = cum8.transpose(0, 2, 1)
        mix_fox = _fox_sample(proj, lay, B, T, l, past["fox_k"], past["fox_v"], past["fox_lf_t"], page_table,
                              jnp.broadcast_to(cum_rows[..., None], (B, H, T, LANES)), cum_rows)

    pos0 = 0 if page_table is None else page_table.shape[1] * past["moba_k"].shape[2]
    mq, mk = _rope(proj, lay, B, T, pos0)
    if page_table is None:
        mix_moba = _moba_prompt(mq, mk, proj, lay, B, T)
    else:
        sel = _moba_sel(mq.reshape(B, T, H, HEAD_DIM), l, past["moba_k"], page_table)
        mix_moba = _moba_sample(mq, mk, proj, lay, B, T, l, past["moba_k"], past["moba_v"], page_table,
                                sel.transpose(0, 2, 1, 3))

    mix_rwkv, rwkv_s = _rwkv(proj, lay, B, T, past["shift"], past["rwkv"], p, l)

    x3 = _out_proj(x3, (mix_gla, mix_fox, mix_moba, mix_rwkv), wl["w_out"], mod3, 2)

    qp, h2 = _norm_mod_matmul(x3, mod3, 4, 3, p["norm_ffn"][l], wl["wq"], emit_h=True, name="peer_q")
    NK = p["peer_keys"].shape[3]
    eidx_t, gate_t = _peer_topk(qp, p["peer_keys"][l])
    bb, tt = _row_tiles(B, T)
    w4 = _peer_w(eidx_t.T, gate_t.T, NK, bb * tt)
    x3 = _peer_dense(x3, h2, w4, wl["u"], wl["v"], mod3, 5)

    def heads(c0):
        return proj[:, c0 * LANES:(c0 + GB) * LANES].reshape(B, T, H, HEAD_DIM)

    last = proj.reshape(B, T, -1)[:, T - 1]
    n_down = p["rwkv_mu"].shape[1] - 3 * G
    shift = jnp.concatenate([last[:, 10 * G:13 * G], last[:, lay["DOWN"] * LANES:lay["DOWN"] * LANES + n_down]], axis=-1)
    new = (heads(5 * GB), heads(6 * GB), lf8, mk.reshape(B, T, H, HEAD_DIM), heads(9 * GB), gla_s, rwkv_s, shift)
    return x3, new


def kernel(x_prompt, x_sample, cache_fox_k, cache_fox_v, cache_fox_logf, cache_moba_k, cache_moba_v, state_gla, state_rwkv, state_rwkv_shift, page_table, c_prompt, c_sample, w_mod, b_mod, norm_mix, norm_ffn, w_in, gla_gate_up, gla_gate_b, gla_norm, fox_forget_b, rwkv_mu, rwkv_w0, rwkv_w_up, rwkv_a0, rwkv_a_up, rwkv_g_up, rwkv_k_k, rwkv_k_a, rwkv_r_k, rwkv_ln_w, rwkv_ln_b, w_out, peer_wq, peer_keys, peer_u, peer_v, final_norm):
    p = dict(norm_mix=norm_mix, norm_ffn=norm_ffn, gla_gate_up=gla_gate_up, gla_gate_b=gla_gate_b,
             gla_norm=gla_norm, rwkv_mu=rwkv_mu, rwkv_w0=rwkv_w0, rwkv_w_up=rwkv_w_up, rwkv_a0=rwkv_a0,
             rwkv_a_up=rwkv_a_up, rwkv_g_up=rwkv_g_up, rwkv_k_k=rwkv_k_k, rwkv_k_a=rwkv_k_a, rwkv_r_k=rwkv_r_k,
             rwkv_ln_w=rwkv_ln_w, rwkv_ln_b=rwkv_ln_b, peer_keys=peer_keys)
    L, D, _ = w_mod.shape
    G = D // 4
    B, T, _ = x_prompt.shape
    DB, DT, _ = x_sample.shape
    H = G // HEAD_DIM
    RH = G // RWKV_HEAD
    gate_rank = gla_gate_up.shape[1]
    n_down = rwkv_mu.shape[1] - 3 * G

    R = -(-(DB + B) // 8) * 8
    c_all = jnp.concatenate([c_sample, c_prompt, jnp.zeros((R - DB - B, D), F32)], axis=0)
    mod = _adaln_mod(c_all, w_mod, b_mod)

    fox_lf_t = cache_fox_logf.transpose(0, 1, 3, 2)
    xp, xs = x_prompt, x_sample
    news_p, news_s = [], []
    for l in range(L):
        wl = {
            "w_in": _pack_w_in(w_in[l], G, gate_rank, H, n_down),
            "w_out": w_out[l].astype(BF16),
            "wq": peer_wq[l].astype(BF16),
            "u": peer_u[l].astype(BF16),
            "v": peer_v[l].astype(BF16),
            "gup": jnp.pad(gla_gate_up[l], ((0, LANES - gate_rank), (0, 0))).astype(BF16),
            "fb_row": jnp.pad(fox_forget_b[l], (gate_rank, LANES - gate_rank - H)).reshape(1, LANES),
        }
        past_p = {"gla": jnp.zeros((B, H, HEAD_DIM, HEAD_DIM), F32),
                  "rwkv": jnp.zeros((B, RH, RWKV_HEAD, RWKV_HEAD), F32),
                  "shift": jnp.zeros((B, 3 * G + n_down), F32)}
        past_s = {"gla": state_gla[l], "rwkv": state_rwkv[l], "shift": state_rwkv_shift[l],
                  "fox_k": cache_fox_k, "fox_v": cache_fox_v, "fox_lf_t": fox_lf_t,
                  "moba_k": cache_moba_k, "moba_v": cache_moba_v}
        xp, new_p = _layer(xp, mod[l, DB:DB + B, None, :], l, p, wl, past_p, None)
        xs, new_s = _layer(xs, mod[l, 0:DB, None, :], l, p, wl, past_s, page_table)
        news_p.append(new_p)
        news_s.append(new_s)
    y_prompt = _final_norm(xp, final_norm)
    y_sample = _final_norm(xs, final_norm)
    stack = lambda news: [jnp.stack([n[i] for n in news]) for i in range(8)]
    return (y_prompt, y_sample, *stack(news_p), *stack(news_s))
```

```python
import functools

import jax
import jax.numpy as jnp
from jax import lax
from jax.experimental import pallas as pl
from jax.experimental.pallas import tpu as pltpu

F32 = jnp.float32
BF16 = jnp.bfloat16
HIGHEST = lax.Precision.HIGHEST

LANES = 128
HEAD_DIM = 128
RWKV_HEAD = 64
N_MOD = 6
RMS_EPS = 1e-6
GLA_TAU = 16.0
GLA_CHUNK = 64
GLA_SUB = 16
MOBA_BLOCK = 256
MOBA_TOPK = 3
ROPE_THETA = 10000.0
RWKV_LN_EPS = 64e-5
PEER_HEADS = 8
PEER_TOPK = 16
NEG = -1e30
VMEM_LIMIT = 52 * 1024 * 1024
PAGES_PER_STEP = 8


def _cp(sem, vmem=VMEM_LIMIT):
    return pltpu.CompilerParams(dimension_semantics=sem, vmem_limit_bytes=vmem)


def _pick(n, pref, mult=LANES):
    best = None
    d = mult
    while d <= min(n, pref):
        if n % d == 0:
            best = d
        d += mult
    return best if best is not None else n


def _row_tiles(B, T):
    if T >= 256:
        return 1, _pick(T, 512, 8)
    return B, T


def _log_sigmoid(z):
    return jnp.minimum(z, 0.0) - jnp.log(1.0 + jnp.exp(-jnp.abs(z)))


def _softplus(z):
    return jnp.maximum(z, 0.0) + jnp.log(1.0 + jnp.exp(-jnp.abs(z)))


def _cumsum_rows(x):
    n = x.shape[0]
    row = lax.broadcasted_iota(jnp.int32, x.shape, 0)
    s = 1
    while s < n:
        x = x + jnp.where(row >= s, pltpu.roll(x, s, axis=0), 0.0)
        s *= 2
    return x


def _dot_nt(a, b, **kw):
    return lax.dot_general(a, b, (((1,), (1,)), ((), ())), preferred_element_type=F32, **kw)


def _dot_tn(a, b):
    return lax.dot_general(a, b, (((0,), (0,)), ((), ())), preferred_element_type=F32)


def _cast_kernel(x_ref, o_ref):
    o_ref[...] = x_ref[...].astype(BF16)


def _layer_bf16(w, l):
    _, R, C = w.shape
    tr = _pick(R, max(8, (2 * 1024 * 1024) // C), 8)
    return pl.pallas_call(
        _cast_kernel,
        grid=(R // tr,),
        in_specs=[pl.BlockSpec((None, tr, C), lambda i: (l, i, 0))],
        out_specs=pl.BlockSpec((tr, C), lambda i: (i, 0)),
        out_shape=jax.ShapeDtypeStruct((R, C), BF16),
        compiler_params=_cp(("parallel",)),
        name="weight_bf16",
    )(w)


def _mod_kernel(c_ref, w_ref, b_ref, o_ref):
    c = c_ref[...]
    ca = (c * jax.nn.sigmoid(c)).astype(BF16)
    o_ref[...] = jnp.dot(ca, w_ref[...].astype(BF16), preferred_element_type=F32) + b_ref[...]


def _adaln_mod(c_all, w_mod, b_mod):
    L, D, N = w_mod.shape
    R = c_all.shape[0]
    tn = _pick(N, 512)
    return pl.pallas_call(
        _mod_kernel,
        grid=(L, N // tn),
        in_specs=[pl.BlockSpec((R, D), lambda l, n: (0, 0)),
                  pl.BlockSpec((None, D, tn), lambda l, n: (l, 0, n)),
                  pl.BlockSpec((None, 1, tn), lambda l, n: (l, 0, n))],
        out_specs=pl.BlockSpec((None, R, tn), lambda l, n: (l, 0, n)),
        out_shape=jax.ShapeDtypeStruct((L, R, N), F32),
        compiler_params=_cp(("parallel", "parallel")),
        name="adaln_mod",
    )(c_all, w_mod, b_mod.reshape(L, 1, N))


def _nm_kernel(x_ref, sc_ref, sh_ref, gain_ref, w_ref, o_ref, *rest, emit_h):
    if emit_h:
        h_out_ref, h_scr = rest
    else:
        (h_scr,) = rest

    @pl.when(pl.program_id(2) == 0)
    def _():
        x = x_ref[...]
        ms = jnp.mean(x * x, axis=-1, keepdims=True)
        y = x * lax.rsqrt(ms + RMS_EPS) * gain_ref[...]
        h = y * (1.0 + sc_ref[...]) + sh_ref[...]
        hb = h.reshape(h_scr.shape).astype(BF16)
        h_scr[...] = hb
        if emit_h:
            h_out_ref[...] = hb

    o_ref[...] = jnp.dot(h_scr[...], w_ref[...], preferred_element_type=F32)


def _norm_mod_matmul(x3, mod3, sc_idx, sh_idx, gain, w, *, emit_h, name):
    B, T, D = x3.shape
    N = w.shape[1]
    bb, tt = _row_tiles(B, T)
    tn = _pick(N, 512)
    M = B * T
    nt = T // tt
    out_shape = [jax.ShapeDtypeStruct((M, N), F32)]
    out_specs = [pl.BlockSpec((bb * tt, tn), lambda b, t, n: (b * nt + t, n))]
    if emit_h:
        out_shape.append(jax.ShapeDtypeStruct((M, D), BF16))
        out_specs.append(pl.BlockSpec((bb * tt, D), lambda b, t, n: (b * nt + t, 0)))
    res = pl.pallas_call(
        functools.partial(_nm_kernel, emit_h=emit_h),
        grid=(B // bb, nt, N // tn),
        in_specs=[pl.BlockSpec((bb, tt, D), lambda b, t, n: (b, t, 0)),
                  pl.BlockSpec((bb, 1, D), lambda b, t, n: (b, 0, sc_idx)),
                  pl.BlockSpec((bb, 1, D), lambda b, t, n: (b, 0, sh_idx)),
                  pl.BlockSpec((1, D), lambda b, t, n: (0, 0)),
                  pl.BlockSpec((D, tn), lambda b, t, n: (0, n))],
        out_specs=out_specs,
        out_shape=out_shape,
        scratch_shapes=[pltpu.VMEM((bb * tt, D), BF16)],
        compiler_params=_cp(("parallel", "parallel", "arbitrary")),
        name=name,
    )(x3, mod3, mod3, gain.reshape(1, D), w)
    return res if emit_h else res[0]


def _op_kernel(x_ref, m0, m1, m2, m3, w_ref, ga_ref, o_ref):
    G = m0.shape[1]
    acc = jnp.dot(m0[...], w_ref[0:G, :], preferred_element_type=F32)
    acc += jnp.dot(m1[...], w_ref[G:2 * G, :], preferred_element_type=F32)
    acc += jnp.dot(m2[...], w_ref[2 * G:3 * G, :], preferred_element_type=F32)
    acc += jnp.dot(m3[...], w_ref[3 * G:4 * G, :], preferred_element_type=F32)
    o_ref[...] = x_ref[...] + ga_ref[...] * acc.reshape(x_ref.shape)


def _out_proj(x3, mixes, w_out, mod3, ga_idx):
    B, T, D = x3.shape
    G = mixes[0].shape[1]
    bb, tt = _row_tiles(B, T)
    tn = _pick(D, 1024)
    nt = T // tt
    nd = D // tn
    mspec = pl.BlockSpec((bb * tt, G), lambda b, t, n: (b * nt + t, 0))
    return pl.pallas_call(
        _op_kernel,
        grid=(B // bb, nt, nd),
        in_specs=[pl.BlockSpec((bb, tt, tn), lambda b, t, n: (b, t, n)),
                  mspec, mspec, mspec, mspec,
                  pl.BlockSpec((D, tn), lambda b, t, n: (0, n)),
                  pl.BlockSpec((bb, 1, tn), lambda b, t, n: (b, 0, ga_idx * nd + n))],
        out_specs=pl.BlockSpec((bb, tt, tn), lambda b, t, n: (b, t, n)),
        out_shape=jax.ShapeDtypeStruct((B, T, D), F32),
        compiler_params=_cp(("parallel", "parallel", "parallel")),
        name="out_proj",
    )(x3, *mixes, w_out, mod3)


def _fnorm_kernel(x_ref, g_ref, o_ref):
    x = x_ref[...]
    ms = jnp.mean(x * x, axis=-1, keepdims=True)
    o_ref[...] = x * lax.rsqrt(ms + RMS_EPS) * g_ref[...]


def _final_norm(x3, gain):
    B, T, D = x3.shape
    M = B * T
    tm = _pick(M, 512, 8)
    y = pl.pallas_call(
        _fnorm_kernel,
        grid=(M // tm,),
        in_specs=[pl.BlockSpec((tm, D), lambda i: (i, 0)), pl.BlockSpec((1, D), lambda i: (0, 0))],
        out_specs=pl.BlockSpec((tm, D), lambda i: (i, 0)),
        out_shape=jax.ShapeDtypeStruct((M, D), F32),
        compiler_params=_cp(("parallel",)),
        name="final_norm",
    )(x3.reshape(M, D), gain.reshape(1, D))
    return y.reshape(B, T, D)


def _gla_head(q, k, v, gg, z, gn, st_scr, b_scr, k_scr, C, SB):
    nsb = C // SB
    q = q * (HEAD_DIM ** -0.5)
    b = _cumsum_rows(_log_sigmoid(z) * (1.0 / GLA_TAU))
    b_scr[...] = b
    k_scr[...] = k

    row1 = lax.broadcasted_iota(jnp.int32, (C, 1), 0)
    blk1 = row1 // SB
    rin1 = row1 - blk1 * SB
    rowc = lax.broadcasted_iota(jnp.int32, (C, C), 0)
    lanec = lax.broadcasted_iota(jnp.int32, (C, C), 1)
    blkc = rowc // SB

    def rows_of(ref, off):
        parts = [jnp.broadcast_to(ref[sb * SB + off:sb * SB + off + 1, :], (SB, HEAD_DIM)) for sb in range(nsb)]
        return parts[0] if nsb == 1 else jnp.concatenate(parts, axis=0)

    A = jnp.zeros((C, C), F32)
    for j in range(SB):
        bj = rows_of(b_scr, j)
        kj = rows_of(k_scr, j)
        e = jnp.exp(jnp.where(rin1 >= j, b - bj, -jnp.inf))
        col = jnp.sum(q * kj * e, axis=-1, keepdims=True)
        A = jnp.where(lanec == blkc * SB + j, col, A)

    if nsb > 1:
        r_own = rows_of(b_scr, SB - 1)
        ktil = k * jnp.exp(r_own - b)
        for sbj in range(nsb - 1):
            rj = b_scr[(sbj + 1) * SB - 1:(sbj + 1) * SB, :]
            qt = q * jnp.exp(jnp.where(blk1 > sbj, b - rj, 0.0))
            kt = jnp.where(blk1 == sbj, ktil, 0.0)
            aj = _dot_nt(qt.astype(BF16), kt.astype(BF16))
            A = A + jnp.where(blkc > sbj, aj, 0.0)

    st = st_scr[...]
    bl = b[C - 1:C, :]
    k2 = k * jnp.exp(bl - b)
    if C >= 16:
        o = jnp.dot(A.astype(BF16), v.astype(BF16), preferred_element_type=F32)
        kv = _dot_tn(v.astype(BF16), k2.astype(BF16))
    else:
        o = A[:, 0:1] * v[0:1, :]
        for s_ in range(1, C):
            o += A[:, s_:s_ + 1] * v[s_:s_ + 1, :]
        zpad = jnp.zeros((16 - C, HEAD_DIM), F32)
        kv = _dot_tn(jnp.concatenate([v, zpad], axis=0).astype(BF16), jnp.concatenate([k2, zpad], axis=0).astype(BF16))
    o += _dot_nt((q * jnp.exp(b)).astype(BF16), st.astype(BF16))
    st_scr[...] = st * jnp.exp(bl) + kv
    on = o * lax.rsqrt(jnp.mean(o * o, axis=-1, keepdims=True) + RMS_EPS) * gn
    return on * (gg * jax.nn.sigmoid(gg))


def _gla_kernel(q_ref, k_ref, v_ref, g_ref, gd_ref, gup_ref, gb_ref, gn_ref, s0_ref,
                o_ref, so_ref, st_scr, b_scr, k_scr, *, C, SB, HB):
    c = pl.program_id(2)

    @pl.when(c == 0)
    def _():
        for i in range(HB):
            st_scr[i] = s0_ref[i].T

    z_all = jnp.dot(gd_ref[...].astype(BF16), gup_ref[...], preferred_element_type=F32) + gb_ref[...]
    for i in range(HB):
        sl = slice(i * HEAD_DIM, (i + 1) * HEAD_DIM)
        out = _gla_head(q_ref[:, sl], k_ref[:, sl], v_ref[:, sl], g_ref[:, sl], z_all[:, sl], gn_ref[:, sl],
                        st_scr.at[i], b_scr.at[i], k_scr.at[i], C, SB)
        o_ref[:, sl] = out.astype(BF16)

    @pl.when(c == pl.num_programs(2) - 1)
    def _():
        for i in range(HB):
            so_ref[i] = st_scr[i].T


def _gla(proj, lay, B, T, gup_pad, gate_b, gla_norm, s0):
    GB = lay["GB"]
    H = GB
    if T % GLA_CHUNK == 0:
        C, SB = GLA_CHUNK, GLA_SUB
    else:
        C, SB = T, T
    nc = T // C
    HB = 4 if H % 4 == 0 else (2 if H % 2 == 0 else 1)
    W = HB * LANES

    def col(base):
        return pl.BlockSpec((C, W), lambda b, h, c: (b * nc + c, base // HB + h))

    mix, s_out = pl.pallas_call(
        functools.partial(_gla_kernel, C=C, SB=SB, HB=HB),
        grid=(B, H // HB, nc),
        in_specs=[col(0), col(GB), col(2 * GB), col(3 * GB),
                  pl.BlockSpec((C, LANES), lambda b, h, c: (b * nc + c, lay["SMALL_A"])),
                  pl.BlockSpec((LANES, W), lambda b, h, c: (0, h)),
                  pl.BlockSpec((1, W), lambda b, h, c: (0, h)),
                  pl.BlockSpec((1, W), lambda b, h, c: (0, h)),
                  pl.BlockSpec((None, HB, HEAD_DIM, HEAD_DIM), lambda b, h, c: (b, h, 0, 0))],
        out_specs=[pl.BlockSpec((C, W), lambda b, h, c: (b * nc + c, h)),
                   pl.BlockSpec((None, HB, HEAD_DIM, HEAD_DIM), lambda b, h, c: (b, h, 0, 0))],
        out_shape=[jax.ShapeDtypeStruct((B * T, H * HEAD_DIM), BF16),
                   jax.ShapeDtypeStruct((B, H, HEAD_DIM, HEAD_DIM), F32)],
        scratch_shapes=[pltpu.VMEM((HB, HEAD_DIM, HEAD_DIM), F32),
                        pltpu.VMEM((HB, C, HEAD_DIM), F32),
                        pltpu.VMEM((HB, C, HEAD_DIM), F32)],
        compiler_params=_cp(("parallel", "parallel", "arbitrary")),
        name="gla",
    )(proj, proj, proj, proj, proj, gup_pad, gate_b.reshape(1, -1), gla_norm.reshape(1, -1), s0)
    return mix, s_out


def _foxcum_kernel(x_ref, fb_ref, lf_ref, cum_ref, carry):
    @pl.when(pl.program_id(1) == 0)
    def _():
        carry[...] = jnp.zeros_like(carry)

    lf = _log_sigmoid(x_ref[...] + fb_ref[...])
    cs = _cumsum_rows(lf) + carry[...]
    lf_ref[...] = lf
    cum_ref[...] = cs
    n = cs.shape[0]
    carry[...] = cs[n - 1:n, :]


def _fox_cum(proj, lay, B, T, fb_row):
    tt = _pick(T, 256, 8)
    nt = T // tt
    spec = pl.BlockSpec((tt, LANES), lambda b, t: (b * nt + t, 0))
    return pl.pallas_call(
        _foxcum_kernel,
        grid=(B, nt),
        in_specs=[pl.BlockSpec((tt, LANES), lambda b, t: (b * nt + t, lay["SMALL_A"])),
                  pl.BlockSpec((1, LANES), lambda b, t: (0, 0))],
        out_specs=[spec, spec],
        out_shape=[jax.ShapeDtypeStruct((B * T, LANES), F32)] * 2,
        scratch_shapes=[pltpu.VMEM((1, LANES), F32)],
        compiler_params=_cp(("parallel", "arbitrary")),
        name="fox_cum",
    )(proj, fb_row)


def _per_query_block(i, nq, tq, T, body, align=1):
    for ii in range(nq):
        nk = min(T, -(-((ii + 1) * tq) // align) * align)
        pl.when(i == ii)(functools.partial(body, nk))


def _fox_prompt_kernel(q_ref, k_ref, v_ref, cq_ref, ck_ref, o_ref, *, tq):
    i = pl.program_id(2)
    T = k_ref.shape[0]

    def body(nk):
        q = (q_ref[...] * (HEAD_DIM ** -0.5)).astype(BF16)
        s = _dot_nt(q, k_ref[0:nk, :].astype(BF16))
        s = s + cq_ref[...] - ck_ref[:, 0:nk]
        row = i * tq + lax.broadcasted_iota(jnp.int32, (tq, nk), 0)
        colk = lax.broadcasted_iota(jnp.int32, (tq, nk), 1)
        s = jnp.where(colk <= row, s, -jnp.inf)
        m = jnp.max(s, axis=-1, keepdims=True)
        p = jnp.exp(s - m)
        l = jnp.sum(p, axis=-1, keepdims=True)
        o = jnp.dot(p.astype(BF16), v_ref[0:nk, :].astype(BF16), preferred_element_type=F32)
        o_ref[...] = (o / l).astype(BF16)

    _per_query_block(i, T // tq, tq, T, body)


def _fox_prompt(proj, lay, B, T, cum_col, cum_row):
    GB = lay["GB"]
    H = GB
    tq = _pick(T, 256, 8)
    nq = T // tq
    return pl.pallas_call(
        functools.partial(_fox_prompt_kernel, tq=tq),
        grid=(B, H, nq),
        in_specs=[pl.BlockSpec((tq, LANES), lambda b, h, i: (b * nq + i, 4 * GB + h)),
                  pl.BlockSpec((T, LANES), lambda b, h, i: (b, 5 * GB + h)),
                  pl.BlockSpec((T, LANES), lambda b, h, i: (b, 6 * GB + h)),
                  pl.BlockSpec((None, None, tq, 1), lambda b, h, i: (b, h, i, 0)),
                  pl.BlockSpec((None, None, 1, T), lambda b, h, i: (b, h, 0, 0))],
        out_specs=pl.BlockSpec((tq, LANES), lambda b, h, i: (b * nq + i, h)),
        out_shape=jax.ShapeDtypeStruct((B * T, H * HEAD_DIM), BF16),
        compiler_params=_cp(("parallel", "parallel", "parallel")),
        name="fox_prompt",
    )(proj, proj, proj, cum_col, cum_row)


def _rope_kernel(q_ref, k_ref, cos_ref, sin_ref, qo_ref, ko_ref):
    cos = cos_ref[...]
    sin = sin_ref[...]
    q = q_ref[...]
    k = k_ref[...]
    qo_ref[...] = q * cos + pltpu.roll(q, HEAD_DIM // 2, axis=1) * sin
    ko_ref[...] = k * cos + pltpu.roll(k, HEAD_DIM // 2, axis=1) * sin


def _rope(proj, lay, B, T, pos0):
    GB = lay["GB"]
    H = GB
    half = HEAD_DIM // 2
    inv = 1.0 / (ROPE_THETA ** (jnp.arange(0, HEAD_DIM, 2, dtype=F32) / HEAD_DIM))
    ang = (pos0 + jnp.arange(T, dtype=jnp.int32)).astype(F32)[:, None] * inv[None, :]
    cos = jnp.concatenate([jnp.cos(ang), jnp.cos(ang)], axis=-1)
    sin = jnp.concatenate([-jnp.sin(ang), jnp.sin(ang)], axis=-1)
    assert cos.shape == (T, 2 * half)
    tt = _pick(T, 512, 8)
    nt = T // tt
    ospec = pl.BlockSpec((tt, LANES), lambda b, t, h: (b * nt + t, h))
    tspec = pl.BlockSpec((tt, LANES), lambda b, t, h: (t, 0))
    return pl.pallas_call(
        _rope_kernel,
        grid=(B, nt, H),
        in_specs=[pl.BlockSpec((tt, LANES), lambda b, t, h: (b * nt + t, 7 * GB + h)),
                  pl.BlockSpec((tt, LANES), lambda b, t, h: (b * nt + t, 8 * GB + h)),
                  tspec, tspec],
        out_specs=[ospec, ospec],
        out_shape=[jax.ShapeDtypeStruct((B * T, H * HEAD_DIM), F32)] * 2,
        compiler_params=_cp(("parallel", "parallel", "parallel")),
        name="rope",
    )(proj, proj, cos, sin)


def _top_lanes(g, avail, n_pick):
    lane = lax.broadcasted_iota(jnp.int32, g.shape, g.ndim - 1)
    sel = jnp.zeros(g.shape, dtype=jnp.bool_)
    for _ in range(n_pick):
        cur = jnp.where(avail, g, -jnp.inf)
        m = jnp.max(cur, axis=-1, keepdims=True)
        first = jnp.min(jnp.where(avail & (cur == m), lane, 1 << 30), axis=-1, keepdims=True)
        pick = avail & (lane == first)
        sel = sel | pick
        avail = avail & jnp.logical_not(pick)
    return sel


def _moba_prompt_kernel(q_ref, k_ref, v_ref, o_ref, *, tq):
    i = pl.program_id(2)
    T = k_ref.shape[0]

    def body(nk):
        NB = nk // MOBA_BLOCK
        q = q_ref[...]
        k = k_ref[0:nk, :]
        kmean = jnp.mean(k.reshape(NB, MOBA_BLOCK, HEAD_DIM), axis=1)
        if NB < LANES:
            kmean = jnp.concatenate([kmean, jnp.zeros((LANES - NB, HEAD_DIM), F32)], axis=0)
        g = _dot_nt(q, kmean, precision=HIGHEST)
        row1 = i * tq + lax.broadcasted_iota(jnp.int32, (tq, 1), 0)
        own1 = row1 // MOBA_BLOCK
        blk = lax.broadcasted_iota(jnp.int32, (tq, LANES), 1)
        sel = _top_lanes(g, blk < own1, MOBA_TOPK)
        expand = (lax.broadcasted_iota(jnp.int32, (LANES, nk), 0)
                  == lax.broadcasted_iota(jnp.int32, (LANES, nk), 1) // MOBA_BLOCK).astype(BF16)
        selk = jnp.dot(sel.astype(BF16), expand, preferred_element_type=F32) > 0.5
        colk = lax.broadcasted_iota(jnp.int32, (tq, nk), 1)
        ok = selk | ((colk // MOBA_BLOCK == own1) & (colk <= row1))
        s = _dot_nt((q * (HEAD_DIM ** -0.5)).astype(BF16), k.astype(BF16))
        s = jnp.where(ok, s, -jnp.inf)
        m = jnp.max(s, axis=-1, keepdims=True)
        p = jnp.exp(s - m)
        l = jnp.sum(p, axis=-1, keepdims=True)
        o = jnp.dot(p.astype(BF16), v_ref[0:nk, :].astype(BF16), preferred_element_type=F32)
        o_ref[...] = (o / l).astype(BF16)

    _per_query_block(i, T // tq, tq, T, body, align=MOBA_BLOCK)


def _moba_prompt(mq, mk, proj, lay, B, T):
    GB = lay["GB"]
    H = GB
    assert T % MOBA_BLOCK == 0 and T // MOBA_BLOCK <= LANES
    tq = _pick(T, 256, 8)
    nq = T // tq
    return pl.pallas_call(
        functools.partial(_moba_prompt_kernel, tq=tq),
        grid=(B, H, nq),
        in_specs=[pl.BlockSpec((tq, LANES), lambda b, h, i: (b * nq + i, h)),
                  pl.BlockSpec((T, LANES), lambda b, h, i: (b, h)),
                  pl.BlockSpec((T, LANES), lambda b, h, i: (b, 9 * GB + h))],
        out_specs=pl.BlockSpec((tq, LANES), lambda b, h, i: (b * nq + i, h)),
        out_shape=jax.ShapeDtypeStruct((B * T, H * HEAD_DIM), BF16),
        compiler_params=_cp(("parallel", "parallel", "parallel")),
        name="moba_prompt",
    )(mq, mk, proj)


def _head_keys(page_refs, h, H):
    PAGE = page_refs[0].shape[0] // H
    parts = [r[pl.ds(h, PAGE, stride=H), :].astype(BF16) for r in page_refs]
    return parts[0] if len(parts) == 1 else jnp.concatenate(parts, axis=0)


def _paged_rows(cache):
    L, NPOOL, PAGE, H, hd = cache.shape
    rows = cache.reshape(L * NPOOL * PAGE * H, hd)

    def spec(layer, page_of):
        return pl.BlockSpec((PAGE * H, hd), lambda b, j, pt: (layer * NPOOL + page_of(b, j, pt), 0))

    return rows, spec


def _paged_step(H, qn_ref, k_refs, v_refs, adjust, m_scr, l_scr, acc_scr):
    nblk = len(k_refs) * (k_refs[0].shape[0] // H) // LANES
    logits = []
    for h in range(H):
        qb = (qn_ref[:, h * HEAD_DIM:(h + 1) * HEAD_DIM] * (HEAD_DIM ** -0.5)).astype(BF16)
        s = _dot_nt(qb, _head_keys(k_refs, h, H))
        logits.append([adjust(h, i, s[:, i * LANES:(i + 1) * LANES]) for i in range(nblk)])
    probs, alphas = [], []
    for h in range(H):
        mx = logits[h][0][0]
        for sb, _ in logits[h][1:]:
            mx = jnp.maximum(mx, sb)
        m_old = m_scr[h]
        m_new = jnp.maximum(m_old, jnp.max(mx, axis=-1, keepdims=True))
        alpha = jnp.exp(m_old - m_new)
        pb = [jnp.exp(sb - m_new) if ok is None else jnp.where(ok, jnp.exp(sb - m_new), 0.0) for sb, ok in logits[h]]
        tot = pb[0]
        for x in pb[1:]:
            tot = tot + x
        m_scr[h] = m_new
        l_scr[h] = alpha * l_scr[h] + jnp.sum(tot, axis=-1, keepdims=True)
        probs.append((pb[0] if nblk == 1 else jnp.concatenate(pb, axis=1)).astype(BF16))
        alphas.append(alpha)
    for h in range(H):
        acc_scr[h] = alphas[h] * acc_scr[h] + jnp.dot(probs[h], _head_keys(v_refs, h, H), preferred_element_type=F32)


def _new_token_columns(h, q, kn_ref, vn_ref, bias_col, bias_row_at, m, l, acc):
    T = q.shape[0]
    trow = lax.broadcasted_iota(jnp.int32, (T, 1), 0)
    cols = []
    for s_ in range(T):
        kn = kn_ref[s_:s_ + 1, h * HEAD_DIM:(h + 1) * HEAD_DIM]
        c = jnp.sum(q * kn, axis=-1, keepdims=True) + bias_col - bias_row_at(s_)
        cols.append(jnp.where(trow >= s_, c, NEG))
    m_new = m
    for c in cols:
        m_new = jnp.maximum(m_new, c)
    alpha = jnp.exp(m - m_new)
    l = alpha * l
    acc = alpha * acc
    for s_, c in enumerate(cols):
        p = jnp.where(trow >= s_, jnp.exp(c - m_new), 0.0)
        l = l + p
        acc = acc + p * vn_ref[s_:s_ + 1, h * HEAD_DIM:(h + 1) * HEAD_DIM]
    return l, acc


def _fox_sample_kernel(pt_ref, *refs, H, PP):
    del pt_ref
    k_refs = refs[0:PP]
    v_refs = refs[PP:2 * PP]
    lf_refs = refs[2 * PP:3 * PP]
    qn_ref, kn_ref, vn_ref, cc_ref, cr_ref, o_ref, m_scr, l_scr, acc_scr, carry = refs[3 * PP:]
    j = pl.program_id(1)
    T = qn_ref.shape[0]
    PAGE = k_refs[0].shape[0] // H

    @pl.when(j == 0)
    def _():
        m_scr[...] = jnp.full(m_scr.shape, NEG, F32)
        l_scr[...] = jnp.zeros_like(l_scr)
        acc_scr[...] = jnp.zeros_like(acc_scr)
        carry[...] = jnp.zeros_like(carry)

    lane = lax.broadcasted_iota(jnp.int32, (H, PAGE), 1)
    biases = []
    run = carry[...]
    for i in range(PP):
        x = lf_refs[i][...]
        y = x
        s_ = 1
        while s_ < PAGE:
            y = y + jnp.where(lane + s_ < PAGE, pltpu.roll(y, PAGE - s_, axis=1), 0.0)
            s_ *= 2
        biases.append(y - x + run)
        run = run + jnp.broadcast_to(y[:, 0:1], (H, PAGE))
    carry[...] = run

    assert PAGE == LANES
    _paged_step(H, qn_ref, k_refs, v_refs,
                lambda h, i, sb: (sb + biases[i][h:h + 1, :] + cc_ref[h], None),
                m_scr, l_scr, acc_scr)

    @pl.when(j == pl.num_programs(1) - 1)
    def _():
        for h in range(H):
            q = qn_ref[:, h * HEAD_DIM:(h + 1) * HEAD_DIM] * (HEAD_DIM ** -0.5)
            l, acc = _new_token_columns(
                h, q, kn_ref, vn_ref, cc_ref[h], lambda s_: cr_ref[h:h + 1, s_:s_ + 1],
                m_scr[h], l_scr[h], acc_scr[h])
            o_ref[:, h * HEAD_DIM:(h + 1) * HEAD_DIM] = (acc / l).astype(BF16)


def _fox_sample(proj, lay, DB, T, layer, cache_k, cache_v, cache_lf_t, page_table, cum_cols, cum_rows):
    GB = lay["GB"]
    H = GB
    G = H * HEAD_DIM
    PAGE = cache_k.shape[2]
    n_pages = page_table.shape[1]
    PP = PAGES_PER_STEP if n_pages % PAGES_PER_STEP == 0 else 1
    NS = n_pages // PP
    k_rows, page_spec = _paged_rows(cache_k)
    v_rows, _ = _paged_rows(cache_v)

    def kv_spec(i):
        return page_spec(layer, lambda b, j, pt: pt[b, n_pages - 1 - (j * PP + i)])

    def lf_spec(i):
        return pl.BlockSpec((None, None, H, PAGE),
                            lambda b, j, pt: (layer, pt[b, n_pages - 1 - (j * PP + i)], 0, 0))

    def new_spec(idx):
        return pl.BlockSpec((T, G), lambda b, j, pt: (b, idx))

    grid_spec = pltpu.PrefetchScalarGridSpec(
        num_scalar_prefetch=1,
        grid=(DB, NS),
        in_specs=([kv_spec(i) for i in range(PP)] + [kv_spec(i) for i in range(PP)]
                  + [lf_spec(i) for i in range(PP)]
                  + [new_spec(4), new_spec(5), new_spec(6),
                     pl.BlockSpec((None, H, T, LANES), lambda b, j, pt: (b, 0, 0, 0)),
                     pl.BlockSpec((None, H, T), lambda b, j, pt: (b, 0, 0))]),
        out_specs=pl.BlockSpec((T, G), lambda b, j, pt: (b, 0)),
        scratch_shapes=[pltpu.VMEM((H, T, LANES), F32), pltpu.VMEM((H, T, LANES), F32),
                        pltpu.VMEM((H, T, HEAD_DIM), F32), pltpu.VMEM((H, PAGE), F32)],
    )
    return pl.pallas_call(
        functools.partial(_fox_sample_kernel, H=H, PP=PP),
        grid_spec=grid_spec,
        out_shape=jax.ShapeDtypeStruct((DB * T, G), BF16),
        compiler_params=_cp(("parallel", "arbitrary")),
        name="fox_sample",
    )(page_table, *([k_rows] * PP), *([v_rows] * PP), *([cache_lf_t] * PP),
      proj, proj, proj, cum_cols, cum_rows)


def _moba_sel_kernel(pt_ref, *refs, H, PP, PPB):
    del pt_ref
    k_refs = refs[0:PP]
    q_ref, sel_ref, g_scr = refs[PP:]
    j = pl.program_id(1)
    T = q_ref.shape[0]
    PAGE = k_refs[0].shape[0]

    @pl.when(j == 0)
    def _():
        g_scr[...] = jnp.zeros_like(g_scr)

    lane = lax.broadcasted_iota(jnp.int32, (H, LANES), 1)
    for bi in range(PP // PPB):
        ks = jnp.sum(k_refs[bi * PPB][...], axis=0)
        for r in range(1, PPB):
            ks = ks + jnp.sum(k_refs[bi * PPB + r][...], axis=0)
        kmean = ks * (1.0 / (PPB * PAGE))
        blk = j * (PP // PPB) + bi
        for t in range(T):
            gcol = jnp.sum(q_ref[t] * kmean, axis=-1, keepdims=True)
            g_scr[t] = jnp.where(lane == blk, gcol, g_scr[t])

    @pl.when(j == pl.num_programs(1) - 1)
    def _():
        n_blk = pl.num_programs(1) * (PP // PPB)
        for t in range(T):
            sel = _top_lanes(g_scr[t], lane < n_blk, MOBA_TOPK)
            sel_ref[t] = sel.astype(F32)


def _moba_sel(mq3, layer, cache_k, page_table):
    DB, T, H, _ = mq3.shape
    PAGE = cache_k.shape[2]
    n_pages = page_table.shape[1]
    PPB = MOBA_BLOCK // PAGE
    PP = PAGES_PER_STEP if n_pages % PAGES_PER_STEP == 0 else PPB
    assert MOBA_BLOCK % PAGE == 0 and PP % PPB == 0 and n_pages % PP == 0 and n_pages // PPB <= LANES
    NS = n_pages // PP

    def k_spec(i):
        return pl.BlockSpec((None, None, PAGE, H, HEAD_DIM),
                            lambda b, j, pt: (layer, pt[b, j * PP + i], 0, 0, 0))

    grid_spec = pltpu.PrefetchScalarGridSpec(
        num_scalar_prefetch=1,
        grid=(DB, NS),
        in_specs=[k_spec(i) for i in range(PP)]
                 + [pl.BlockSpec((None, T, H, HEAD_DIM), lambda b, j, pt: (b, 0, 0, 0))],
        out_specs=pl.BlockSpec((None, T, H, LANES), lambda b, j, pt: (b, 0, 0, 0)),
        scratch_shapes=[pltpu.VMEM((T, H, LANES), F32)],
    )
    return pl.pallas_call(
        functools.partial(_moba_sel_kernel, H=H, PP=PP, PPB=PPB),
        grid_spec=grid_spec,
        out_shape=jax.ShapeDtypeStruct((DB, T, H, LANES), F32),
        compiler_params=_cp(("parallel", "arbitrary")),
        name="moba_sel",
    )(page_table, *([cache_k] * PP), mq3)


def _moba_sample_kernel(pt_ref, *refs, H, PP, PPB):
    del pt_ref
    k_refs = refs[0:PP]
    v_refs = refs[PP:2 * PP]
    qn_ref, kn_ref, vn_ref, sel_ref, o_ref, m_scr, l_scr, acc_scr = refs[2 * PP:]
    j = pl.program_id(1)
    T = qn_ref.shape[0]
    PAGE = k_refs[0].shape[0] // H
    W = PP * PAGE

    @pl.when(j == 0)
    def _():
        m_scr[...] = jnp.full(m_scr.shape, NEG, F32)
        l_scr[...] = jnp.zeros_like(l_scr)
        acc_scr[...] = jnp.zeros_like(acc_scr)

    expand = (lax.broadcasted_iota(jnp.int32, (LANES, W), 0)
              == j * (PP // PPB) + lax.broadcasted_iota(jnp.int32, (LANES, W), 1) // (PPB * PAGE)).astype(BF16)
    oks = [jnp.dot(sel_ref[h].astype(BF16), expand, preferred_element_type=F32) > 0.5 for h in range(H)]

    def adjust(h, i, sb):
        ok = oks[h][:, i * LANES:(i + 1) * LANES]
        return jnp.where(ok, sb, NEG), ok

    _paged_step(H, qn_ref, k_refs, v_refs, adjust, m_scr, l_scr, acc_scr)

    @pl.when(j == pl.num_programs(1) - 1)
    def _():
        zero = jnp.zeros((1, 1), F32)
        for h in range(H):
            q = qn_ref[:, h * HEAD_DIM:(h + 1) * HEAD_DIM] * (HEAD_DIM ** -0.5)
            l, acc = _new_token_columns(h, q, kn_ref, vn_ref, zero, lambda s_: zero,
                                        m_scr[h], l_scr[h], acc_scr[h])
            o_ref[:, h * HEAD_DIM:(h + 1) * HEAD_DIM] = (acc / l).astype(BF16)


def _moba_sample(mq, mk, proj, lay, DB, T, layer, cache_k, cache_v, page_table, sel):
    GB = lay["GB"]
    H = GB
    G = H * HEAD_DIM
    PAGE = cache_k.shape[2]
    n_pages = page_table.shape[1]
    PPB = MOBA_BLOCK // PAGE
    PP = PAGES_PER_STEP if n_pages % PAGES_PER_STEP == 0 else PPB
    NS = n_pages // PP
    assert (n_pages * PAGE) % MOBA_BLOCK == 0 and T <= MOBA_BLOCK

    k_rows, page_spec = _paged_rows(cache_k)
    v_rows, _ = _paged_rows(cache_v)

    def kv_spec(i):
        return page_spec(layer, lambda b, j, pt: pt[b, j * PP + i])

    grid_spec = pltpu.PrefetchScalarGridSpec(
        num_scalar_prefetch=1,
        grid=(DB, NS),
        in_specs=([kv_spec(i) for i in range(PP)] + [kv_spec(i) for i in range(PP)]
                  + [pl.BlockSpec((T, G), lambda b, j, pt: (b, 0)),
                     pl.BlockSpec((T, G), lambda b, j, pt: (b, 0)),
                     pl.BlockSpec((T, G), lambda b, j, pt: (b, 9)),
                     pl.BlockSpec((None, H, T, LANES), lambda b, j, pt: (b, 0, 0, 0))]),
        out_specs=pl.BlockSpec((T, G), lambda b, j, pt: (b, 0)),
        scratch_shapes=[pltpu.VMEM((H, T, LANES), F32), pltpu.VMEM((H, T, LANES), F32),
                        pltpu.VMEM((H, T, HEAD_DIM), F32)],
    )
    return pl.pallas_call(
        functools.partial(_moba_sample_kernel, H=H, PP=PP, PPB=PPB),
        grid_spec=grid_spec,
        out_shape=jax.ShapeDtypeStruct((DB * T, G), BF16),
        compiler_params=_cp(("parallel", "arbitrary")),
        name="moba_sample",
    )(page_table, *([k_rows] * PP), *([v_rows] * PP), mq, mk, proj, sel)


RW_R, RW_W, RW_K, RW_KK, RW_KKA, RW_WR, RW_N = 0, 1, 2, 3, 4, 5, 6


def _seg_sum(x):
    bd = (lax.broadcasted_iota(jnp.int32, (LANES, LANES), 0) // RWKV_HEAD
          == lax.broadcasted_iota(jnp.int32, (LANES, LANES), 1) // RWKV_HEAD).astype(F32)
    outs = [jnp.dot(x[:, c * LANES:(c + 1) * LANES], bd, precision=HIGHEST, preferred_element_type=F32)
            for c in range(x.shape[1] // LANES)]
    return outs[0] if len(outs) == 1 else jnp.concatenate(outs, axis=1)


def _rwkv_prep_kernel(r_ref, k_ref, v_ref, d_ref, rp_ref, kp_ref, vp_ref, dp_ref,
                      s0m_ref, s0d_ref, mum_ref, mud_ref,
                      wup_ref, aup_ref, gup_ref, w0_ref, a0_ref, kk_ref, ka_ref, rk_ref,
                      kvo, vo, go, bo):
    t = pl.program_id(1)
    tt, G = r_ref.shape
    first = t == 0

    def mixed(cur_ref, prev_ref, s0, mu):
        cur = cur_ref[...]
        n = cur.shape[0]
        prow = jnp.where(first, s0, prev_ref[prev_ref.shape[0] - 1:, :])
        row = lax.broadcasted_iota(jnp.int32, cur.shape, 0)
        prev = jnp.where(row == 0, prow, pltpu.roll(cur, 1, axis=0)) if n > 1 else prow
        return cur + (prev - cur) * mu

    r = mixed(r_ref, rp_ref, s0m_ref[:, 0:G], mum_ref[:, 0:G])
    kr = mixed(k_ref, kp_ref, s0m_ref[:, G:2 * G], mum_ref[:, G:2 * G])
    vr = mixed(v_ref, vp_ref, s0m_ref[:, 2 * G:3 * G], mum_ref[:, 2 * G:3 * G])
    d = mixed(d_ref, dp_ref, s0d_ref[...], mud_ref[...])

    dwa = d[:, 0:LANES]
    tw = jnp.dot(jnp.tanh(dwa).astype(BF16), wup_ref[...], preferred_element_type=F32)
    w_raw = -_softplus(-(w0_ref[...] + tw)) - 0.5
    w = jnp.exp(-jnp.exp(w_raw))
    a = jax.nn.sigmoid(a0_ref[...] + jnp.dot(dwa.astype(BF16), aup_ref[...], preferred_element_type=F32))
    g = jnp.dot(jax.nn.sigmoid(d[:, LANES:]).astype(BF16), gup_ref[...], preferred_element_type=F32)
    kk = kr * kk_ref[...]
    kk = kk / jnp.maximum(jnp.sqrt(_seg_sum(kk * kk)), 1e-12)
    k2 = kr * (1.0 + (a - 1.0) * ka_ref[...])
    kvo[RW_R] = r
    kvo[RW_W] = w
    kvo[RW_K] = k2
    kvo[RW_KK] = kk
    kvo[RW_KKA] = kk * a
    kvo[RW_WR] = w * r
    vo[...] = vr
    go[...] = g
    bo[...] = _seg_sum(r * k2 * rk_ref[...]) * vr


def _rwkv_prep(proj, lay, B, T, shift_main, shift_down, mu_main, mu_down, wts):
    GB = lay["GB"]
    G = GB * LANES
    tt = _pick(T, 256, 8)
    nt = T // tt
    dcol = lay["DOWN"]
    assert (dcol * LANES) % (3 * LANES) == 0
    dblk = dcol // 3

    def cur(idx, w):
        return pl.BlockSpec((tt, w), lambda b, t: (b * nt + t, idx))

    def prev(idx, w):
        return pl.BlockSpec((8, w), lambda b, t: (jnp.maximum((b * T + t * tt) // 8 - 1, 0), idx))

    full = lambda a: pl.BlockSpec(a.shape, lambda b, t: (0,) * a.ndim)
    ospec = pl.BlockSpec((tt, G), lambda b, t: (b * nt + t, 0))
    outs = pl.pallas_call(
        _rwkv_prep_kernel,
        grid=(B, nt),
        in_specs=[cur(10, G), cur(11, G), cur(12, G), cur(dblk, 3 * LANES),
                  prev(10, G), prev(11, G), prev(12, G), prev(dblk, 3 * LANES),
                  pl.BlockSpec((None, 1, 3 * G), lambda b, t: (b, 0, 0)),
                  pl.BlockSpec((None, 1, 3 * LANES), lambda b, t: (b, 0, 0)),
                  full(mu_main), full(mu_down)] + [full(a) for a in wts],
        out_specs=[pl.BlockSpec((RW_N, tt, G), lambda b, t: (0, b * nt + t, 0))] + [ospec] * 3,
        out_shape=[jax.ShapeDtypeStruct((RW_N, B * T, G), F32)] + [jax.ShapeDtypeStruct((B * T, G), F32)] * 3,
        compiler_params=_cp(("parallel", "arbitrary")),
        name="rwkv_prep",
    )(proj, proj, proj, proj, proj, proj, proj, proj, shift_main, shift_down, mu_main, mu_down, *wts)
    return outs


def _rwkv_scan_kernel(kv_ref, v_ref, s0_ref, y_ref, so_ref, s_scr, *, KH):
    i = pl.program_id(0)
    _, tt, K, L = kv_ref.shape
    assert KH & (KH - 1) == 0 and L % KH == 0

    @pl.when(i == 0)
    def _():
        s_scr[...] = s0_ref[...]

    def row(a, t, kidx):
        return kv_ref[a, t, pl.ds(kidx, 1), :]

    def fold(x):
        s = L // 2
        while s >= L // KH:
            x = x + pltpu.roll(x, s, axis=1)
            s //= 2
        return x

    vshape = v_ref.shape[1:]
    sa0 = jnp.zeros(vshape, F32)
    ya0 = jnp.zeros(vshape, F32)
    for kidx in range(K):
        s = s_scr[kidx]
        sa0 = sa0 - s * row(RW_KK, 0, kidx)
        ya0 = ya0 + s * row(RW_WR, 0, kidx)
    sa0 = fold(sa0)
    ya0 = fold(ya0)

    def token(t, carry):
        sa, ya = carry
        vv = v_ref[t]
        rt = kv_ref[RW_R, t]
        kr_dot = jnp.sum(fold(kv_ref[RW_K, t] * rt), axis=0, keepdims=True)
        kar_dot = jnp.sum(fold(kv_ref[RW_KKA, t] * rt), axis=0, keepdims=True)
        y_ref[t] = ya + sa * kar_dot + vv * kr_dot
        tn = jnp.minimum(t + 1, tt - 1)
        sa_n = jnp.zeros(vshape, F32)
        ya_n = jnp.zeros(vshape, F32)
        for kidx in range(K):
            s = s_scr[kidx] * row(RW_W, t, kidx) + sa * row(RW_KKA, t, kidx) + vv * row(RW_K, t, kidx)
            s_scr[kidx] = s
            sa_n = sa_n - s * row(RW_KK, tn, kidx)
            ya_n = ya_n + s * row(RW_WR, tn, kidx)
        return fold(sa_n), fold(ya_n)

    lax.fori_loop(0, tt, token, (sa0, ya0))

    @pl.when(i == pl.num_programs(0) - 1)
    def _():
        so_ref[...] = s_scr[...]


def _rwkv_scan(kv, v, s0, KH):
    _, T, K, L = kv.shape
    VP = v.shape[1]
    tt = _pick(T, 32, 1)
    vspec = pl.BlockSpec((tt, VP, L), lambda i: (i, 0, 0))
    sspec = pl.BlockSpec((K, VP, L), lambda i: (0, 0, 0))
    return pl.pallas_call(
        functools.partial(_rwkv_scan_kernel, KH=KH),
        grid=(T // tt,),
        in_specs=[pl.BlockSpec((RW_N, tt, K, L), lambda i: (0, i, 0, 0)), vspec, sspec],
        out_specs=[vspec, sspec],
        out_shape=[jax.ShapeDtypeStruct((T, VP, L), F32), jax.ShapeDtypeStruct((K, VP, L), F32)],
        scratch_shapes=[pltpu.VMEM((K, VP, L), F32)],
        compiler_params=_cp(("arbitrary",)),
        name="rwkv_scan",
    )(kv, v, s0)


def _rwkv_post_kernel(y_ref, b_ref, g_ref, lw_ref, lb_ref, o_ref):
    y = y_ref[...]
    mu = _seg_sum(y) * (1.0 / RWKV_HEAD)
    yc = y - mu
    var = _seg_sum(yc * yc) * (1.0 / RWKV_HEAD)
    yn = yc * lax.rsqrt(var + RWKV_LN_EPS) * lw_ref[...] + lb_ref[...]
    o_ref[...] = ((yn + b_ref[...]) * g_ref[...]).astype(BF16)


def _rwkv_post(y, bonus, g, ln_w, ln_b):
    M, G = y.shape
    tm = _pick(M, 512, 8)
    spec = pl.BlockSpec((tm, G), lambda i: (i, 0))
    rspec = pl.BlockSpec((1, G), lambda i: (0, 0))
    return pl.pallas_call(
        _rwkv_post_kernel,
        grid=(M // tm,),
        in_specs=[spec, spec, spec, rspec, rspec],
        out_specs=spec,
        out_shape=jax.ShapeDtypeStruct((M, G), BF16),
        compiler_params=_cp(("parallel",)),
        name="rwkv_post",
    )(y, bonus, g, ln_w.reshape(1, G), ln_b.reshape(1, G))


def _rwkv(proj, lay, B, T, shift0, state0, p, l):
    GB = lay["GB"]
    G = GB * LANES
    RH = G // RWKV_HEAD
    N = RWKV_HEAD
    chains = B * RH
    assert LANES % chains == 0 or chains % LANES == 0
    VH = max(1, LANES // chains)
    VP = N // VH
    rw = p["rwkv_w_up"][l].shape[0]
    ra = p["rwkv_a_up"][l].shape[0]
    rg = p["rwkv_g_up"][l].shape[0]
    assert rw + ra == LANES and rg <= 2 * LANES

    def padrows(a, top, total):
        return jnp.pad(a, ((top, total - top - a.shape[0]), (0, 0))).astype(BF16)

    def pad_down(a):
        return jnp.pad(a, [(0, 0)] * (a.ndim - 1) + [(0, 3 * LANES - a.shape[-1])])

    wts = (padrows(p["rwkv_w_up"][l], 0, LANES), padrows(p["rwkv_a_up"][l], rw, LANES),
           padrows(p["rwkv_g_up"][l], 0, 2 * LANES),
           p["rwkv_w0"][l].reshape(1, G), p["rwkv_a0"][l].reshape(1, G),
           p["rwkv_k_k"][l].reshape(1, G), p["rwkv_k_a"][l].reshape(1, G), p["rwkv_r_k"][l].reshape(1, G))
    mu = p["rwkv_mu"][l]
    kv, v, g, bonus = _rwkv_prep(
        proj, lay, B, T,
        shift0[:, None, :3 * G], pad_down(shift0[:, None, 3 * G:]),
        mu[None, :3 * G], pad_down(mu[None, 3 * G:]), wts)

    KH, KP = VH, VP
    kv_t = kv.reshape(RW_N, B, T, RH, KH, KP).transpose(0, 2, 5, 4, 1, 3).reshape(RW_N, T, KP, KH * chains)
    v_t = jnp.tile(v.reshape(B, T, RH, N).transpose(1, 3, 0, 2).reshape(T, N, chains), (1, 1, KH))
    s0_t = state0.reshape(B, RH, N, KH, KP).transpose(4, 2, 3, 0, 1).reshape(KP, N, KH * chains)
    y_t, s_t = _rwkv_scan(kv_t, v_t, s0_t, KH)
    y = y_t[:, :, :chains].reshape(T, N, B, RH).transpose(2, 0, 3, 1).reshape(B * T, G)
    s_new = s_t.reshape(KP, N, KH, B, RH).transpose(3, 4, 1, 2, 0).reshape(B, RH, N, N)
    mix = _rwkv_post(y, bonus, g, p["rwkv_ln_w"][l], p["rwkv_ln_b"][l])
    return mix, s_new


def _peer_pairs():
    return [(a, PEER_TOPK // (a + 1)) for a in range(PEER_TOPK)]


def _peer_topk_kernel(q_ref, keys_ref, e_ref, g_ref, v1_scr, i1_scr, v2_scr, i2_scr, c_scr, ci_scr, t_scr, te_scr):
    NK = keys_ref.shape[2]
    KH = keys_ref.shape[3]
    tn = q_ref.shape[0]
    TK = PEER_TOPK
    big = jnp.int32(1 << 30)

    def top_rows(s, iota, vals_scr, idx_scr, payload=None):
        for kk in range(TK):
            m = jnp.max(s, axis=0, keepdims=True)
            first = jnp.min(jnp.where(s == m, iota, big), axis=0, keepdims=True)
            hit = iota == first
            vals_scr[kk:kk + 1, :] = m
            if payload is None:
                idx_scr[kk:kk + 1, :] = first
            else:
                idx_scr[kk:kk + 1, :] = jnp.max(jnp.where(hit, payload, -1), axis=0, keepdims=True)
            s = jnp.where(hit, -jnp.inf, s)

    iota_k = lax.broadcasted_iota(jnp.int32, (NK, tn), 0)
    ncp = c_scr.shape[0]
    iota_c = lax.broadcasted_iota(jnp.int32, (ncp, tn), 0)
    for h in range(PEER_HEADS):
        for half, (vs, is_) in enumerate(((v1_scr, i1_scr), (v2_scr, i2_scr))):
            c0 = (h * 2 + half) * KH
            s = _dot_nt(keys_ref[h, half], q_ref[:, c0:c0 + KH], precision=HIGHEST)
            top_rows(s, iota_k, vs, is_)
        v2 = v2_scr[...]
        i2 = i2_scr[...]
        c_scr[ncp - 8:ncp, :] = jnp.full((8, tn), -jnp.inf, F32)
        ci_scr[ncp - 8:ncp, :] = jnp.zeros((8, tn), jnp.int32)
        off = 0
        for a, n in _peer_pairs():
            c_scr[off:off + n, :] = v1_scr[a:a + 1, :] + v2[0:n, :]
            ci_scr[off:off + n, :] = i1_scr[a:a + 1, :] * NK + i2[0:n, :]
            off += n
        top_rows(c_scr[...], iota_c, t_scr, te_scr, payload=ci_scr[...])
        top = t_scr[...]
        ex = jnp.exp(top - top[0:1, :])
        g_ref[h * TK:(h + 1) * TK, :] = ex / jnp.sum(ex, axis=0, keepdims=True)
        e_ref[h * TK:(h + 1) * TK, :] = te_scr[...]


def _peer_topk(qp, keys):
    M = qp.shape[0]
    NK = keys.shape[2]
    tn = _pick(M, 128)
    J = PEER_HEADS * PEER_TOPK
    TK = PEER_TOPK
    ncp = -(-sum(n for _, n in _peer_pairs()) // 8) * 8
    return pl.pallas_call(
        _peer_topk_kernel,
        grid=(M // tn,),
        in_specs=[pl.BlockSpec((tn, qp.shape[1]), lambda i: (i, 0)),
                  pl.BlockSpec(keys.shape, lambda i: (0, 0, 0, 0))],
        out_specs=[pl.BlockSpec((J, tn), lambda i: (0, i))] * 2,
        out_shape=[jax.ShapeDtypeStruct((J, M), jnp.int32), jax.ShapeDtypeStruct((J, M), F32)],
        scratch_shapes=[pltpu.VMEM((TK, tn), F32), pltpu.VMEM((TK, tn), jnp.int32),
                        pltpu.VMEM((TK, tn), F32), pltpu.VMEM((TK, tn), jnp.int32),
                        pltpu.VMEM((ncp, tn), F32), pltpu.VMEM((ncp, tn), jnp.int32),
                        pltpu.VMEM((TK, tn), F32), pltpu.VMEM((TK, tn), jnp.int32)],
        compiler_params=_cp(("parallel",)),
        name="peer_topk",
    )(qp, keys)


def _peer_w_kernel(e0_ref, g0_ref, e1_ref, g1_ref, w_ref, *, NK):
    tp, J = e0_ref.shape
    iota = lax.broadcasted_iota(jnp.int32, (NK, J), 0)
    shift = NK.bit_length() - 1
    assert 1 << shift == NK
    sub = lax.broadcasted_iota(jnp.int32, (NK // 8, 8, NK), 1)

    def table(e_ref, g_ref, n):
        e = e_ref[pl.ds(n, 1), :]
        g = g_ref[pl.ds(n, 1), :]
        a = e >> shift
        b = e & (NK - 1)
        at = jnp.where(iota == a, g, 0.0).astype(BF16)
        bt = (iota == b).astype(BF16)
        return _dot_nt(at, bt).reshape(NK // 8, 8, NK)

    def pair(m, carry):
        w0 = table(e0_ref, g0_ref, m)
        w1 = table(e1_ref, g1_ref, m)
        lo = jnp.where(sub < 4, w0, pltpu.roll(w1, 4, axis=1))
        hi = jnp.where(sub < 4, pltpu.roll(w0, 4, axis=1), w1)
        w_ref[:, pl.ds(pl.multiple_of(m * 8, 8), 8), :] = jnp.stack([lo, hi], axis=1).reshape(NK // 4, 8, NK)
        return carry

    lax.fori_loop(0, tp, pair, 0, unroll=4)


def _peer_w(eidx, gates, NK, tw):
    M, J = eidx.shape
    half = tw // 2
    tp = _pick(half, 32, 8)
    assert NK % 8 == 0 and tw % 2 == 0 and M % tw == 0
    nc = half // tp
    first = pl.BlockSpec((tp, J), lambda i, c: (i * 2 * nc + c, 0))
    second = pl.BlockSpec((tp, J), lambda i, c: (i * 2 * nc + nc + c, 0))
    return pl.pallas_call(
        functools.partial(_peer_w_kernel, NK=NK),
        grid=(M // tw, nc),
        in_specs=[first, first, second, second],
        out_specs=pl.BlockSpec((NK // 4, tp * 8, NK), lambda i, c: (0, i * nc + c, 0)),
        out_shape=jax.ShapeDtypeStruct((NK // 4, M // 2 * 8, NK), F32),
        compiler_params=_cp(("parallel", "parallel")),
        name="peer_w",
    )(eidx, gates, eidx, gates)


def _peer_dense_kernel(h_ref, u_ref, w_ref, v_ref, x_ref, ga_ref, o_ref):
    e = pl.program_id(2)

    @pl.when(e == 0)
    def _():
        o_ref[...] = jnp.zeros_like(o_ref)

    a = _dot_nt(h_ref[...], u_ref[...])
    act = 0.5 * a * (1.0 + lax.erf(a * (2.0 ** -0.5)))
    NK = w_ref.shape[1]
    half = w_ref.shape[0] // 8
    zs = []
    for r in range(4):
        wr = jnp.concatenate([w_ref[pl.ds(r, half, stride=8), :], w_ref[pl.ds(4 + r, half, stride=8), :]], axis=0)
        zs.append((act[:, r * NK:(r + 1) * NK] * wr).astype(BF16))
    z = jnp.concatenate(zs, axis=1)
    o_ref[...] += jnp.dot(z, v_ref[...], preferred_element_type=F32).reshape(o_ref.shape)

    @pl.when(e == pl.num_programs(2) - 1)
    def _():
        o_ref[...] = x_ref[...] + ga_ref[...] * o_ref[...]


def _peer_dense(x3, h2, w4, u, v, mod3, ga_idx):
    B, T, D = x3.shape
    E = u.shape[0]
    bb, tt = _row_tiles(B, T)
    NK = w4.shape[2]
    te = 4 * NK
    nt = T // tt
    half = bb * tt // 2
    assert w4.shape == (NK // 4, B * T * 4, NK) and E == NK * NK
    once = pl.Buffered(1)
    return pl.pallas_call(
        _peer_dense_kernel,
        grid=(B // bb, nt, E // te),
        in_specs=[pl.BlockSpec((bb * tt, D), lambda b, t, e: (b * nt + t, 0), pipeline_mode=once),
                  pl.BlockSpec((te, D), lambda b, t, e: (e, 0)),
                  pl.BlockSpec((None, half * 8, NK), lambda b, t, e: (e, b * nt + t, 0)),
                  pl.BlockSpec((te, D), lambda b, t, e: (e, 0)),
                  pl.BlockSpec((bb, tt, D), lambda b, t, e: (b, t, 0), pipeline_mode=once),
                  pl.BlockSpec((bb, 1, D), lambda b, t, e: (b, 0, ga_idx))],
        out_specs=pl.BlockSpec((bb, tt, D), lambda b, t, e: (b, t, 0)),
        out_shape=jax.ShapeDtypeStruct((B, T, D), F32),
        compiler_params=_cp(("parallel", "parallel", "arbitrary")),
        name="peer_dense",
    )(h2, u, w4, v, x3, mod3)


def _layout(G):
    GB = G // LANES
    return {"GB": GB, "SMALL_A": 13 * GB, "DOWN": 13 * GB + 1, "NP": (13 * GB + 4) * LANES}


def _pack_w_in(w_in, G, gate_rank, n_fox_heads, n_down):
    gla, fox = 0, 4 * G + gate_rank
    moba = fox + 3 * G + n_fox_heads
    rw = moba + 3 * G
    D = w_in.shape[0]
    zeros = lambda n: jnp.zeros((D, n), w_in.dtype)
    parts = [w_in[:, gla:gla + 4 * G], w_in[:, fox:fox + 3 * G], w_in[:, moba:moba + 3 * G], w_in[:, rw:rw + 3 * G],
             w_in[:, 4 * G:4 * G + gate_rank], w_in[:, fox + 3 * G:fox + 3 * G + n_fox_heads],
             zeros(LANES - gate_rank - n_fox_heads),
             w_in[:, rw + 3 * G:rw + 3 * G + n_down], zeros(3 * LANES - n_down)]
    return jnp.concatenate(parts, axis=1).astype(BF16)


def _layer(x3, mod3, l, p, wl, past, page_table):
    B, T, D = x3.shape
    G = D // 4
    lay = _layout(G)
    GB = lay["GB"]
    H = GB
    gate_rank = p["gla_gate_up"].shape[1]
    M = B * T

    proj = _norm_mod_matmul(x3, mod3, 1, 0, p["norm_mix"][l], wl["w_in"], emit_h=False, name="in_proj")

    mix_gla, gla_s = _gla(proj, lay, B, T, wl["gup"], p["gla_gate_b"][l], p["gla_norm"][l], past["gla"])

    lf, cum = _fox_cum(proj, lay, B, T, wl["fb_row"])
    lf8 = lf[:, gate_rank:gate_rank + H].reshape(B, T, H)
    cum8 = cum[:, gate_rank:gate_rank + H].reshape(B, T, H)
    if page_table is None:
        mix_fox = _fox_prompt(proj, lay, B, T, cum8.transpose(0, 2, 1)[..., None], cum8.transpose(0, 2, 1)[:, :, None, :])
    else:
        cum_rows = cum8.transpose(0, 2, 1)
        mix_fox = _fox_sample(proj, lay, B, T, l, past["fox_k"], past["fox_v"], past["fox_lf_t"], page_table,
                              jnp.broadcast_to(cum_rows[..., None], (B, H, T, LANES)), cum_rows)

    pos0 = 0 if page_table is None else page_table.shape[1] * past["moba_k"].shape[2]
    mq, mk = _rope(proj, lay, B, T, pos0)
    if page_table is None:
        mix_moba = _moba_prompt(mq, mk, proj, lay, B, T)
    else:
        sel = _moba_sel(mq.reshape(B, T, H, HEAD_DIM), l, past["moba_k"], page_table)
        mix_moba = _moba_sample(mq, mk, proj, lay, B, T, l, past["moba_k"], past["moba_v"], page_table,
                                sel.transpose(0, 2, 1, 3))

    mix_rwkv, rwkv_s = _rwkv(proj, lay, B, T, past["shift"], past["rwkv"], p, l)

    x3 = _out_proj(x3, (mix_gla, mix_fox, mix_moba, mix_rwkv), wl["w_out"], mod3, 2)

    qp, h2 = _norm_mod_matmul(x3, mod3, 4, 3, p["norm_ffn"][l], wl["wq"], emit_h=True, name="peer_q")
    NK = p["peer_keys"].shape[3]
    eidx_t, gate_t = _peer_topk(qp, p["peer_keys"][l])
    bb, tt = _row_tiles(B, T)
    w4 = _peer_w(eidx_t.T, gate_t.T, NK, bb * tt)
    x3 = _peer_dense(x3, h2, w4, wl["u"], wl["v"], mod3, 5)

    def heads(c0):
        return proj[:, c0 * LANES:(c0 + GB) * LANES].reshape(B, T, H, HEAD_DIM)

    last = proj.reshape(B, T, -1)[:, T - 1]
    n_down = p["rwkv_mu"].shape[1] - 3 * G
    shift = jnp.concatenate([last[:, 10 * G:13 * G], last[:, lay["DOWN"] * LANES:lay["DOWN"] * LANES + n_down]], axis=-1)
    new = (heads(5 * GB), heads(6 * GB), lf8, mk.reshape(B, T, H, HEAD_DIM), heads(9 * GB), gla_s, rwkv_s, shift)
    return x3, new


def kernel(x_prompt, x_sample, cache_fox_k, cache_fox_v, cache_fox_logf, cache_moba_k, cache_moba_v, state_gla, state_rwkv, state_rwkv_shift, page_table, c_prompt, c_sample, w_mod, b_mod, norm_mix, norm_ffn, w_in, gla_gate_up, gla_gate_b, gla_norm, fox_forget_b, rwkv_mu, rwkv_w0, rwkv_w_up, rwkv_a0, rwkv_a_up, rwkv_g_up, rwkv_k_k, rwkv_k_a, rwkv_r_k, rwkv_ln_w, rwkv_ln_b, w_out, peer_wq, peer_keys, peer_u, peer_v, final_norm):
    p = dict(norm_mix=norm_mix, norm_ffn=norm_ffn, gla_gate_up=gla_gate_up, gla_gate_b=gla_gate_b,
             gla_norm=gla_norm, rwkv_mu=rwkv_mu, rwkv_w0=rwkv_w0, rwkv_w_up=rwkv_w_up, rwkv_a0=rwkv_a0,
             rwkv_a_up=rwkv_a_up, rwkv_g_up=rwkv_g_up, rwkv_k_k=rwkv_k_k, rwkv_k_a=rwkv_k_a, rwkv_r_k=rwkv_r_k,
             rwkv_ln_w=rwkv_ln_w, rwkv_ln_b=rwkv_ln_b, peer_keys=peer_keys)
    L, D, _ = w_mod.shape
    G = D // 4
    B, T, _ = x_prompt.shape
    DB, DT, _ = x_sample.shape
    H = G // HEAD_DIM
    RH = G // RWKV_HEAD
    gate_rank = gla_gate_up.shape[1]
    n_down = rwkv_mu.shape[1] - 3 * G

    R = -(-(DB + B) // 8) * 8
    c_all = jnp.concatenate([c_sample, c_prompt, jnp.zeros((R - DB - B, D), F32)], axis=0)
    mod = _adaln_mod(c_all, w_mod, b_mod)

    fox_lf_t = cache_fox_logf.transpose(0, 1, 3, 2)
    xp, xs = x_prompt, x_sample
    news_p, news_s = [], []
    for l in range(L):
        wl = {
            "w_in": _pack_w_in(w_in[l], G, gate_rank, H, n_down),
            "w_out": _layer_bf16(w_out, l),
            "wq": _layer_bf16(peer_wq, l),
            "u": _layer_bf16(peer_u, l),
            "v": _layer_bf16(peer_v, l),
            "gup": jnp.pad(gla_gate_up[l], ((0, LANES - gate_rank), (0, 0))).astype(BF16),
            "fb_row": jnp.pad(fox_forget_b[l], (gate_rank, LANES - gate_rank - H)).reshape(1, LANES),
        }
        past_p = {"gla": jnp.zeros((B, H, HEAD_DIM, HEAD_DIM), F32),
                  "rwkv": jnp.zeros((B, RH, RWKV_HEAD, RWKV_HEAD), F32),
                  "shift": jnp.zeros((B, 3 * G + n_down), F32)}
        past_s = {"gla": state_gla[l], "rwkv": state_rwkv[l], "shift": state_rwkv_shift[l],
                  "fox_k": cache_fox_k, "fox_v": cache_fox_v, "fox_lf_t": fox_lf_t,
                  "moba_k": cache_moba_k, "moba_v": cache_moba_v}
        xp, new_p = _layer(xp, mod[l, DB:DB + B, None, :], l, p, wl, past_p, None)
        xs, new_s = _layer(xs, mod[l, 0:DB, None, :], l, p, wl, past_s, page_table)
        news_p.append(new_p)
        news_s.append(new_s)
    y_prompt = _final_norm(xp, final_norm)
    y_sample = _final_norm(xs, final_norm)
    stack = lambda news: [jnp.stack([n[i] for n in news]) for i in range(8)]
    return (y_prompt, y_sample, *stack(news_p), *stack(news_s))
```
